```python
import jax, jax.numpy as jnp
from jax import lax
import numpy as np

D_MODEL = 1024
BATCH = 4
SEQ = 4096
DEPTH = 1

C_CONV = 512
CONV_WIDTH = 31
RW_HEADS = 8
RW_HEAD_DIM = 64
RW_WIDTH = RW_HEADS * RW_HEAD_DIM
LORA_W = 32
LORA_A = 32
LORA_G = 96
N_BRANCH = 2
PK_HEADS = 8
PK_DIM = 256
PK_HALF = PK_DIM // 2
N_KEYS = 128
N_EXPERTS = N_KEYS * N_KEYS
PK_TOPK = 16
EXPERT_BLOCK = 128
NORM_EPS = 1e-6
LN_EPS = 1e-5
GN_EPS = 64e-5
COL_CONV = 2 * C_CONV
COL_RKV = 3 * RW_WIDTH
COL_LORA = LORA_W + LORA_A + LORA_G
COL_GATE = N_BRANCH * D_MODEL
D_IN = COL_CONV + COL_RKV + COL_LORA + COL_GATE

kernel_name = 'hybrid_conv_rwkv7_peer_block'


def rmsnorm(x, g):
    xf = x.astype(jnp.float32)
    y = xf * lax.rsqrt(jnp.mean(xf * xf, axis=-1, keepdims=True) + NORM_EPS)
    return (y * g.astype(jnp.float32)).astype(x.dtype)


def token_shift_lerp(z, mu):
    prev = jnp.pad(z, ((0, 0), (1, 0), (0, 0)))[:, :-1]
    return z + (prev - z) * mu


def conv_branch(u, dw, dw_b, ln_g, ln_b, pw, pw_b):
    a, b = jnp.split(u, 2, axis=-1)
    glu = a * jax.nn.sigmoid(b)
    y = lax.conv_general_dilated(
        glu, dw.reshape(CONV_WIDTH, 1, C_CONV).astype(glu.dtype),
        window_strides=(1,), padding=[(CONV_WIDTH - 1, 0)],
        dimension_numbers=('NWC', 'WIO', 'NWC'),
        feature_group_count=C_CONV) + dw_b
    yf = y.astype(jnp.float32)
    mean = jnp.mean(yf, axis=-1, keepdims=True)
    var = jnp.mean(jnp.square(yf - mean), axis=-1, keepdims=True)
    yn = (yf - mean) * lax.rsqrt(var + LN_EPS) * ln_g.astype(jnp.float32) + ln_b.astype(jnp.float32)
    yn = jax.nn.silu(yn).astype(u.dtype)
    return yn @ pw + pw_b


def rwkv7_branch(u, mu, w0, w2, a0, a2, g2, k_k, k_a, r_k, gn_g, gn_b, w_o):
    bsz, s = u.shape[0], u.shape[1]
    u = token_shift_lerp(u, mu)
    r, k, v, lw, la, lg = jnp.split(
        u, [RW_WIDTH, 2 * RW_WIDTH, 3 * RW_WIDTH,
            3 * RW_WIDTH + LORA_W, 3 * RW_WIDTH + LORA_W + LORA_A], axis=-1)
    w = -jax.nn.softplus(-(w0 + jnp.tanh(lw) @ w2)) - 0.5
    decay = jnp.exp(-jnp.exp(w.astype(jnp.float32)))
    a = jax.nn.sigmoid(a0 + la @ a2)
    g = jax.nn.sigmoid(lg) @ g2
    kk = k * k_k
    k = k * (1 + (a - 1) * k_a)

    def heads(t):
        return t.reshape(bsz, s, RW_HEADS, RW_HEAD_DIM).astype(jnp.float32)

    r_h, k_h, v_h, a_h, w_h, kk_h = map(heads, (r, k, v, a, decay, kk))
    kk_h = kk_h / jnp.maximum(jnp.linalg.norm(kk_h, axis=-1, keepdims=True), 1e-12)

    def step(state, inp):
        r_t, w_t, k_t, v_t, kk_t, a_t = inp
        sa = jnp.einsum('bhvk,bhk->bhv', state, -kk_t)
        state = (state * w_t[:, :, None, :]
                 + sa[..., None] * (kk_t * a_t)[:, :, None, :]
                 + v_t[..., None] * k_t[:, :, None, :])
        return state, jnp.einsum('bhvk,bhk->bhv', state, r_t)

    def time_major(t):
        return jnp.moveaxis(t, 1, 0)

    s0 = jnp.zeros((bsz, RW_HEADS, RW_HEAD_DIM, RW_HEAD_DIM), jnp.float32)
    _, y = lax.scan(step, s0, (time_major(r_h), time_major(w_h), time_major(k_h),
                               time_major(v_h), time_major(kk_h), time_major(a_h)))
    y = jnp.moveaxis(y, 0, 1)
    mean = jnp.mean(y, axis=-1, keepdims=True)
    var = jnp.mean(jnp.square(y - mean), axis=-1, keepdims=True)
    yn = ((y - mean) * lax.rsqrt(var + GN_EPS)
          * gn_g.reshape(RW_HEADS, RW_HEAD_DIM).astype(jnp.float32)
          + gn_b.reshape(RW_HEADS, RW_HEAD_DIM).astype(jnp.float32))
    bonus = jnp.sum(r_h * k_h * r_k.astype(jnp.float32), axis=-1, keepdims=True) * v_h
    o = (yn + bonus).reshape(bsz, s, RW_WIDTH).astype(u.dtype) * g
    return o @ w_o


def peer_ffn(h, w_q, keys, u_tab, v_tab):
    bsz, s, d = h.shape
    q = (h @ w_q).reshape(bsz, s, PK_HEADS, PK_DIM).astype(jnp.float32)
    kf = keys.astype(jnp.float32)
    s1 = jnp.einsum('bshd,hnd->bshn', q[..., :PK_HALF], kf[:, 0])
    s2 = jnp.einsum('bshd,hnd->bshn', q[..., PK_HALF:], kf[:, 1])
    v1, i1 = lax.top_k(s1, PK_TOPK)
    v2, i2 = lax.top_k(s2, PK_TOPK)
    cand = (v1[..., :, None] + v2[..., None, :]).reshape(bsz, s, PK_HEADS, PK_TOPK * PK_TOPK)
    best, c = lax.top_k(cand, PK_TOPK)
    e1 = jnp.take_along_axis(i1, c // PK_TOPK, axis=-1)
    e2 = jnp.take_along_axis(i2, c % PK_TOPK, axis=-1)
    expert = e1 * N_KEYS + e2
    gate = jax.nn.softmax(best, axis=-1)
    n_tok = bsz * s
    nb = n_tok // EXPERT_BLOCK
    hk = PK_HEADS * PK_TOPK
    xb = h.reshape(nb, EXPERT_BLOCK, d)
    eb = expert.reshape(nb, EXPERT_BLOCK, hk)
    gb = gate.reshape(nb, EXPERT_BLOCK, hk).astype(h.dtype)

    def block(args):
        xt, et, gt = args
        u_sel = jnp.take(u_tab, et, axis=0)
        act = jax.nn.gelu(jnp.einsum('tkd,td->tk', u_sel, xt), approximate=False)
        v_sel = jnp.take(v_tab, et, axis=0)
        return jnp.einsum('tk,tkd->td', gt * act, v_sel)

    out = lax.map(block, (xb, eb, gb))
    return out.reshape(bsz, s, d)


def setup_inputs(seed: int = 0) -> dict:
    key = jax.random.key(seed)
    ks = jax.random.split(key, 40)
    f32 = jnp.float32
    L = DEPTH

    def nrm(k, shape, std):
        return jax.random.normal(k, shape, f32) * std

    def ones_noise(k, shape, base=1.0):
        return base + 0.05 * jax.random.normal(k, shape, f32)

    return {
        'x': jax.random.normal(ks[0], (BATCH, SEQ, D_MODEL), f32),
        'norm_mix_g': ones_noise(ks[1], (L, D_MODEL)),
        'w_in': nrm(ks[2], (L, D_MODEL, D_IN), D_MODEL ** -0.5),
        'conv_dw': nrm(ks[3], (L, CONV_WIDTH, C_CONV), CONV_WIDTH ** -0.5),
        'conv_dw_b': nrm(ks[4], (L, C_CONV), 0.02),
        'conv_ln_g': ones_noise(ks[5], (L, C_CONV)),
        'conv_ln_b': nrm(ks[6], (L, C_CONV), 0.02),
        'conv_pw': nrm(ks[7], (L, C_CONV, D_MODEL), C_CONV ** -0.5),
        'conv_pw_b': nrm(ks[8], (L, D_MODEL), 0.02),
        'rw_mu': jax.random.uniform(ks[9], (L, COL_RKV + COL_LORA), f32),
        'rw_w0': jax.random.uniform(ks[10], (L, RW_WIDTH), f32, -5.0, -1.0),
        'rw_w2': nrm(ks[11], (L, LORA_W, RW_WIDTH), 0.3 * LORA_W ** -0.5),
        'rw_a0': nrm(ks[12], (L, RW_WIDTH), 0.1),
        'rw_a2': nrm(ks[13], (L, LORA_A, RW_WIDTH), LORA_A ** -0.5),
        'rw_g2': nrm(ks[14], (L, LORA_G, RW_WIDTH), LORA_G ** -0.5),
        'rw_k_k': ones_noise(ks[15], (L, RW_WIDTH), 0.85),
        'rw_k_a': ones_noise(ks[16], (L, RW_WIDTH)),
        'rw_r_k': nrm(ks[17], (L, RW_HEADS, RW_HEAD_DIM), 0.1),
        'rw_gn_g': ones_noise(ks[18], (L, RW_WIDTH)),
        'rw_gn_b': nrm(ks[19], (L, RW_WIDTH), 0.02),
        'rw_w_o': nrm(ks[20], (L, RW_WIDTH, D_MODEL), RW_WIDTH ** -0.5),
        'w_out': nrm(ks[21], (L, D_MODEL, D_MODEL), D_MODEL ** -0.5),
        'norm_ffn_g': ones_noise(ks[22], (L, D_MODEL)),
        'peer_wq': nrm(ks[23], (L, D_MODEL, PK_HEADS * PK_DIM), D_MODEL ** -0.5),
        'peer_keys': nrm(ks[24], (L, PK_HEADS, 2, N_KEYS, PK_HALF), PK_HALF ** -0.5),
        'peer_u': nrm(ks[25], (L, N_EXPERTS, D_MODEL), D_MODEL ** -0.5),
        'peer_v': nrm(ks[26], (L, N_EXPERTS, D_MODEL), 0.25),
        'norm_f_g': ones_noise(ks[27], (D_MODEL,)),
    }


def reference(x, norm_mix_g, w_in, conv_dw, conv_dw_b, conv_ln_g, conv_ln_b, conv_pw, conv_pw_b,
              rw_mu, rw_w0, rw_w2, rw_a0, rw_a2, rw_g2, rw_k_k, rw_k_a, rw_r_k, rw_gn_g, rw_gn_b,
              rw_w_o, w_out, norm_ffn_g, peer_wq, peer_keys, peer_u, peer_v, norm_f_g):
    for l in range(DEPTH):
        h = rmsnorm(x, norm_mix_g[l])
        proj = h @ w_in[l]
        u_conv, u_rw, gate_logits = jnp.split(
            proj, [COL_CONV, COL_CONV + COL_RKV + COL_LORA], axis=-1)
        y_conv = conv_branch(u_conv, conv_dw[l], conv_dw_b[l], conv_ln_g[l], conv_ln_b[l],
                             conv_pw[l], conv_pw_b[l])
        y_rw = rwkv7_branch(u_rw, rw_mu[l], rw_w0[l], rw_w2[l], rw_a0[l], rw_a2[l], rw_g2[l],
                            rw_k_k[l], rw_k_a[l], rw_r_k[l], rw_gn_g[l], rw_gn_b[l], rw_w_o[l])
        g_conv, g_rw = jnp.split(jax.nn.sigmoid(gate_logits), 2, axis=-1)
        x = x + (g_conv * y_conv + g_rw * y_rw) @ w_out[l]
        h = rmsnorm(x, norm_ffn_g[l])
        x = x + peer_ffn(h, peer_wq[l], peer_keys[l], peer_u[l], peer_v[l])
    return rmsnorm(x, norm_f_g)
```

```python
import functools

import jax
import jax.numpy as jnp
from jax import lax
from jax.experimental import pallas as pl
from jax.experimental.pallas import tpu as pltpu

F32 = jnp.float32
BF16 = jnp.bfloat16
HI = lax.Precision.HIGHEST

LANES = 128
SUBLANES = 8

C_CONV = 512
CONV_WIDTH = 31
RW_HEADS = 8
RW_HEAD_DIM = 64
RW_WIDTH = RW_HEADS * RW_HEAD_DIM
LORA_W, LORA_A, LORA_G = 32, 32, 96
PK_HEADS = 8
PK_HALF = 128
N_KEYS = 128
PK_TOPK = 16
NORM_EPS = 1e-6
LN_EPS = 1e-5
GN_EPS = 64e-5

RW_LW = 3 * RW_WIDTH
RW_LA = RW_LW + LANES
RW_LG = RW_LA + LANES
RW_IN = RW_LG + LANES

CHUNK = 64
VMEM_LIMIT = 56 * 1024 * 1024


def _cparams(sem):
    return pltpu.CompilerParams(dimension_semantics=sem, vmem_limit_bytes=VMEM_LIMIT)


def _dot(a, b, dims=(((1,), (0,)), ((), ())), precision=None):
    return lax.dot_general(a, b, dims, precision=precision, preferred_element_type=F32)


_NT = (((1,), (1,)), ((), ()))
_TN = (((0,), (0,)), ((), ()))


def _split2(x):
    hi = x.astype(BF16)
    lo = (x - hi.astype(F32)).astype(BF16)
    return hi, lo


def _split3(x):
    hi = x.astype(BF16)
    r1 = x - hi.astype(F32)
    mid = r1.astype(BF16)
    lo = (r1 - mid.astype(F32)).astype(BF16)
    return hi, mid, lo


def _sigmoid(x):
    return 1.0 / (1.0 + jnp.exp(-x))


def _rwkv_kernel(u_ref, mu_ref, w0_ref, w2_ref, a0_ref, a2_ref, g2_ref, kk_ref, ka_ref,
                 rk_ref, gng_ref, gnb_ref, wo_ref, bd_ref, o_ref, carry_ref, state_ref):
    s = pl.program_id(1)

    @pl.when(s == 0)
    def _():
        carry_ref[...] = jnp.zeros_like(carry_ref)
        state_ref[...] = jnp.zeros_like(state_ref)

    c = CHUNK
    u = u_ref[...]
    row = lax.broadcasted_iota(jnp.int32, u.shape, 0)
    prev = jnp.where(row == 0, carry_ref[...], pltpu.roll(u, 1, axis=0))
    carry_ref[...] = u[c - 1:c, :]
    u = u + (prev - u) * mu_ref[...]

    r = u[:, 0:RW_WIDTH]
    k = u[:, RW_WIDTH:2 * RW_WIDTH]
    v = u[:, 2 * RW_WIDTH:3 * RW_WIDTH]
    lw = u[:, RW_LW:RW_LW + LANES]
    la = u[:, RW_LA:RW_LA + LANES]
    lg = u[:, RW_LG:RW_LG + LANES]

    bd = bd_ref[...]

    def head_sum(x):
        hi, lo = _split2(x)
        return _dot(hi, bd) + _dot(lo, bd)

    z = w0_ref[...] + _dot(jnp.tanh(lw), w2_ref[...])
    sp = jnp.maximum(-z, 0.0) + jnp.log(1.0 + jnp.exp(-jnp.abs(z)))
    wlog = -jnp.exp(-sp - 0.5)
    a_lr = _sigmoid(a0_ref[...] + _dot(la, a2_ref[...]))
    g = _dot(_sigmoid(lg), g2_ref[...])
    kk = k * kk_ref[...]
    kn = kk / jnp.maximum(jnp.sqrt(head_sum(kk * kk)), 1e-12)
    km = k * (1.0 + (a_lr - 1.0) * ka_ref[...])

    ri = lax.broadcasted_iota(jnp.int32, (c, c), 0)
    ci = lax.broadcasted_iota(jnp.int32, (c, c), 1)
    tri = (ri >= ci).astype(BF16)
    w_hi, w_mid, w_lo = _split3(wlog)
    g_incl = _dot(tri, w_hi) + _dot(tri, w_mid) + _dot(tri, w_lo)
    e_incl = jnp.exp(g_incl)
    e_excl = jnp.exp(g_incl - wlog)
    e_neg = jnp.exp(-g_incl)
    r_t = r * e_incl
    a_t = -kn * e_excl
    b_t = kn * a_lr * e_neg
    k_t = km * e_neg
    gamma = e_incl[c - 1:c, :]

    strict = ri > ci
    lower = ri >= ci
    eye = (ri == ci).astype(F32)

    ys = []
    for h in range(RW_HEADS):
        sl = slice(h * RW_HEAD_DIM, (h + 1) * RW_HEAD_DIM)
        ah, bh, rh, kh, vh = a_t[:, sl], b_t[:, sl], r_t[:, sl], k_t[:, sl], v[:, sl]
        a_ab = jnp.where(strict, _dot(ah, bh, _NT, HI), 0.0)
        a_ak = jnp.where(strict, _dot(ah, kh, _NT, HI), 0.0)
        a_rb = jnp.where(lower, _dot(rh, bh, _NT, HI), 0.0)
        a_rk = jnp.where(lower, _dot(rh, kh, _NT, HI), 0.0)
        p = a_ab
        t = eye + p
        for _ in range(5):
            p = _dot(p, p, precision=HI)
            t = t + _dot(t, p, precision=HI)
        st = state_ref[h]
        w = _dot(ah, st, _NT, HI) + _dot(a_ak, vh, precision=HI)
        uu = _dot(t, w, precision=HI)
        y = _dot(rh, st, _NT, HI) + _dot(a_rb, uu, precision=HI) + _dot(a_rk, vh, precision=HI)
        st_new = (st + _dot(uu, bh, _TN, HI) + _dot(vh, kh, _TN, HI)) * gamma[:, sl]
        state_ref[h] = st_new
        ys.append(y)
    y = jnp.concatenate(ys, axis=1)

    inv_n = 1.0 / RW_HEAD_DIM
    mean = head_sum(y) * inv_n
    d = y - mean
    var = head_sum(d * d) * inv_n
    yn = d * lax.rsqrt(var + GN_EPS) * gng_ref[...] + gnb_ref[...]
    bonus = head_sum(r * km * rk_ref[...]) * v
    o = ((yn + bonus) * g).astype(BF16)
    o_ref[...] = _dot(o, wo_ref[...])


def _rwkv_branch(u_rw, mu_p, w0, w2p, a0, a2p, g2p, k_k, k_a, r_k, gn_g, gn_b, w_o_b, bd_b):
    b, s, _ = u_rw.shape
    d_model = w_o_b.shape[1]
    row = lambda w: pl.BlockSpec((1, w), lambda i, j: (0, 0))
    full = lambda a: pl.BlockSpec(a.shape, lambda i, j: (0,) * a.ndim)
    return pl.pallas_call(
        _rwkv_kernel,
        grid=(b, s // CHUNK),
        in_specs=[
            pl.BlockSpec((None, CHUNK, RW_IN), lambda i, j: (i, j, 0)),
            row(RW_IN), row(RW_WIDTH), full(w2p), row(RW_WIDTH), full(a2p), full(g2p),
            row(RW_WIDTH), row(RW_WIDTH), row(RW_WIDTH), row(RW_WIDTH), row(RW_WIDTH),
            full(w_o_b), full(bd_b),
        ],
        out_specs=pl.BlockSpec((None, CHUNK, d_model), lambda i, j: (i, j, 0)),
        out_shape=jax.ShapeDtypeStruct((b, s, d_model), F32),
        scratch_shapes=[
            pltpu.VMEM((1, RW_IN), F32),
            pltpu.VMEM((RW_HEADS, RW_HEAD_DIM, RW_HEAD_DIM), F32),
        ],
        compiler_params=_cparams(("parallel", "arbitrary")),
        name="rwkv7_mix",
    )(u_rw, mu_p, w0, w2p, a0, a2p, g2p, k_k, k_a, r_k, gn_g, gn_b, w_o_b, bd_b)


def _pack_rwkv_params(rw_mu, rw_w2, rw_a2, rw_g2):
    mu = jnp.zeros((1, RW_IN), F32)
    mu = mu.at[0, :RW_LW].set(rw_mu[:RW_LW])
    mu = mu.at[0, RW_LW:RW_LW + LORA_W].set(rw_mu[RW_LW:RW_LW + LORA_W])
    mu = mu.at[0, RW_LA:RW_LA + LORA_A].set(rw_mu[RW_LW + LORA_W:RW_LW + LORA_W + LORA_A])
    mu = mu.at[0, RW_LG:RW_LG + LORA_G].set(rw_mu[RW_LW + LORA_W + LORA_A:])
    pad = lambda w: jnp.zeros((LANES, RW_WIDTH), F32).at[:w.shape[0]].set(w)
    return mu, pad(rw_w2), pad(rw_a2), pad(rw_g2)


def _head_block_ones():
    i = jnp.arange(RW_WIDTH) // RW_HEAD_DIM
    return (i[:, None] == i[None, :]).astype(BF16)


TOPK_TOKENS = 256
NEG_INF = float("-inf")


def _top16(s, payload=None):
    rows = s.shape[0]
    iota = lax.broadcasted_iota(jnp.int32, s.shape, 0)
    r16 = lax.broadcasted_iota(jnp.int32, (PK_TOPK, s.shape[1]), 0)
    vals = jnp.zeros((PK_TOPK, s.shape[1]), F32)
    outs = jnp.zeros((PK_TOPK, s.shape[1]), jnp.int32)
    for it in range(PK_TOPK):
        m = jnp.max(s, axis=0, keepdims=True)
        ix = jnp.min(jnp.where(s == m, iota, rows), axis=0, keepdims=True)
        hit = iota == ix
        if payload is None:
            o = ix
        else:
            o = jnp.max(jnp.where(hit, payload, -1), axis=0, keepdims=True)
        vals = jnp.where(r16 == it, m, vals)
        outs = jnp.where(r16 == it, o, outs)
        s = jnp.where(hit, NEG_INF, s)
    return vals, outs


def _topk_kernel(q_ref, keys_ref, idx_ref, gate_ref):
    k1 = keys_ref[0]
    k2 = keys_ref[1]
    for lt in range(TOPK_TOKENS // LANES):
        rows = slice(lt * LANES, (lt + 1) * LANES)
        q = q_ref[rows, :]
        s1 = _dot(k1, q[:, :PK_HALF], _NT)
        s2 = _dot(k2, q[:, PK_HALF:], _NT)
        v1, i1 = _top16(s1)
        v2, i2 = _top16(s2)
        cand = jnp.concatenate([v1[a:a + 1] + v2[:PK_TOPK // (a + 1)] for a in range(PK_TOPK)], axis=0)
        eid = jnp.concatenate([i1[a:a + 1] * N_KEYS + i2[:PK_TOPK // (a + 1)] for a in range(PK_TOPK)], axis=0)
        pad = -cand.shape[0] % SUBLANES
        cand = jnp.concatenate([cand, jnp.full((pad, LANES), NEG_INF, F32)], axis=0)
        eid = jnp.concatenate([eid, jnp.zeros((pad, LANES), jnp.int32)], axis=0)
        best, e = _top16(cand, eid)
        ex = jnp.exp(best - best[0:1])
        gate = ex / jnp.sum(ex, axis=0, keepdims=True)
        idx_ref[:, rows] = e * TABLE_ROWS_PER_EXPERT
        gate_ref[:, rows] = gate


def _peer_topk(q_b, keys_b):
    n = q_b.shape[0]
    hk = PK_HEADS * PK_TOPK
    return pl.pallas_call(
        _topk_kernel,
        grid=(n // TOPK_TOKENS, PK_HEADS),
        in_specs=[
            pl.BlockSpec((TOPK_TOKENS, 2 * PK_HALF), lambda i, h: (i, h)),
            pl.BlockSpec((None, 2, N_KEYS, PK_HALF), lambda i, h: (h, 0, 0, 0)),
        ],
        out_specs=[
            pl.BlockSpec((PK_TOPK, TOPK_TOKENS), lambda i, h: (h, i)),
            pl.BlockSpec((PK_TOPK, TOPK_TOKENS), lambda i, h: (h, i)),
        ],
        out_shape=[jax.ShapeDtypeStruct((hk, n), jnp.int32), jax.ShapeDtypeStruct((hk, n), F32)],
        compiler_params=_cparams(("parallel", "arbitrary")),
        name="peer_topk",
    )(q_b, keys_b)


PEER_TOKENS = 128
PEER_PAIRS = PK_HEADS * PK_TOPK
TABLE_ROWS_PER_EXPERT = 4
TILE_STRIDE = 136


PACK_EXPERTS = 512


def _pack_kernel(t_ref, o_ref):
    half = TABLE_ROWS_PER_EXPERT * LANES
    for j in range(TABLE_ROWS_PER_EXPERT):
        lo = t_ref[:, j * LANES:(j + 1) * LANES]
        hi = t_ref[:, half + j * LANES:half + (j + 1) * LANES]
        o_ref[pl.ds(j, PACK_EXPERTS, stride=TABLE_ROWS_PER_EXPERT), :] = (
            pltpu.pack_elementwise([lo, hi], packed_dtype=BF16))


def _pack_table(tab):
    e, d = tab.shape
    return pl.pallas_call(
        _pack_kernel,
        grid=(e // PACK_EXPERTS,),
        in_specs=[pl.BlockSpec((PACK_EXPERTS, d), lambda i: (i, 0))],
        out_specs=pl.BlockSpec((PACK_EXPERTS * TABLE_ROWS_PER_EXPERT, LANES), lambda i: (i, 0)),
        out_shape=jax.ShapeDtypeStruct((e * TABLE_ROWS_PER_EXPERT, LANES), jnp.uint32),
        compiler_params=_cparams(("parallel",)),
        name="pack_table",
    )(tab)


def _gather_rows(idx_ref, t, tab_ref, tile_ref):
    row = idx_ref.at[t]
    for m in range(PEER_PAIRS):
        i = pl.multiple_of(row[m], TABLE_ROWS_PER_EXPERT)
        tile_ref[pl.ds(m, TABLE_ROWS_PER_EXPERT, stride=TILE_STRIDE), :] = (
            tab_ref[pl.ds(i, TABLE_ROWS_PER_EXPERT), :])


def _unpack_chunk(tile_ref, j):
    w = tile_ref[pl.ds(j * TILE_STRIDE, PEER_PAIRS), :]
    lo = pltpu.unpack_elementwise(w, index=0, packed_dtype=BF16, unpacked_dtype=F32)
    hi = pltpu.unpack_elementwise(w, index=1, packed_dtype=BF16, unpacked_dtype=F32)
    return lo, hi


def _rms(x, g):
    return x * lax.rsqrt(jnp.mean(x * x, axis=-1, keepdims=True) + NORM_EPS) * g


def _peer_act_kernel(idx_ref, gate_ref, x_ref, g_ref, tab_ref, c_ref, tile_ref, h_ref):
    h_ref[...] = _rms(x_ref[...], g_ref[...])
    gate = gate_ref[...]
    lane = lax.broadcasted_iota(jnp.int32, (PEER_PAIRS, PEER_TOKENS), 1)
    half = TABLE_ROWS_PER_EXPERT * LANES

    def group(t8, acc):
        base = pl.multiple_of(t8 * SUBLANES, SUBLANES)
        h8 = h_ref[pl.ds(base, SUBLANES), :]
        for tt in range(SUBLANES):
            t = base + tt
            tile = tile_ref.at[tt]
            _gather_rows(idx_ref, t, tab_ref, tile)
            part = jnp.zeros((PEER_PAIRS, LANES), F32)
            for j in range(TABLE_ROWS_PER_EXPERT):
                lo, hi = _unpack_chunk(tile, j)
                xlo = h8[tt:tt + 1, j * LANES:(j + 1) * LANES]
                xhi = h8[tt:tt + 1, half + j * LANES:half + (j + 1) * LANES]
                part = part + lo * xlo + hi * xhi
            act = jnp.sum(part, axis=1, keepdims=True)
            hit = lane == t
            gcol = jnp.sum(jnp.where(hit, gate, 0.0), axis=1, keepdims=True)
            coef = gcol * (0.5 * act * (1.0 + lax.erf(act * (2.0 ** -0.5))))
            acc = jnp.where(hit, coef, acc)
        return acc

    acc = lax.fori_loop(0, PEER_TOKENS // SUBLANES, group, jnp.zeros((PEER_PAIRS, PEER_TOKENS), F32))
    c_ref[...] = acc.T


def _peer_out_kernel(idx_ref, c_ref, x_ref, g_ref, tab_ref, o_ref, tile_ref, p_ref, cev_ref, cod_ref):
    cb = c_ref[...].astype(BF16)
    pi = lax.broadcasted_iota(jnp.int32, (PEER_PAIRS, 2 * PEER_PAIRS), 0)
    li = lax.broadcasted_iota(jnp.int32, (PEER_PAIRS, 2 * PEER_PAIRS), 1)
    cev_ref[...] = _dot(cb, (li == 2 * pi).astype(BF16))
    cod_ref[...] = _dot(cb, (li == 2 * pi + 1).astype(BF16))
    sub = lax.broadcasted_iota(jnp.int32, (SUBLANES, LANES), 0)
    nch = TABLE_ROWS_PER_EXPERT

    def group(t8, carry):
        base = pl.multiple_of(t8 * SUBLANES, SUBLANES)
        lhs = jnp.concatenate([cev_ref[pl.ds(base, SUBLANES), :], cod_ref[pl.ds(base, SUBLANES), :]],
                              axis=0).astype(BF16)
        outs = [jnp.zeros((SUBLANES, LANES), F32) for _ in range(2 * nch)]
        for tt in range(SUBLANES):
            tile = tile_ref.at[tt]
            _gather_rows(idx_ref, base + tt, tab_ref, tile)
            for j in range(nch):
                w = pltpu.bitcast(tile[pl.ds(j * TILE_STRIDE, PEER_PAIRS), :], BF16)
                r = _dot(lhs, w)
                outs[j] = jnp.where(sub == tt, r[:SUBLANES], outs[j])
                outs[nch + j] = jnp.where(sub == tt, r[SUBLANES:], outs[nch + j])
        for cidx, o in enumerate(outs):
            p_ref[pl.ds(base, SUBLANES), cidx * LANES:(cidx + 1) * LANES] = o
        return carry

    lax.fori_loop(0, PEER_TOKENS // SUBLANES, group, 0)
    o_ref[...] = _rms(x_ref[...] + p_ref[...], g_ref[...])


def _peer_specs(d_model):
    idx = pl.BlockSpec((PEER_TOKENS, PEER_PAIRS), lambda i: (i, 0), memory_space=pltpu.SMEM)
    col = pl.BlockSpec((PEER_PAIRS, PEER_TOKENS), lambda i: (0, i))
    coef = pl.BlockSpec((PEER_TOKENS, PEER_PAIRS), lambda i: (i, 0))
    tok = pl.BlockSpec((PEER_TOKENS, d_model), lambda i: (i, 0))
    gain = pl.BlockSpec((1, d_model), lambda i: (0, 0))
    tab = pl.BlockSpec(memory_space=pltpu.VMEM)
    tile = pltpu.VMEM((SUBLANES, TABLE_ROWS_PER_EXPERT * TILE_STRIDE, LANES), jnp.uint32)
    return idx, col, coef, tok, gain, tab, tile


def _peer_act(idx, gate_t, x1, g, tab):
    n, d_model = x1.shape
    idx_s, col, coef, tok, gain, tab_s, tile = _peer_specs(d_model)
    return pl.pallas_call(
        _peer_act_kernel,
        grid=(n // PEER_TOKENS,),
        in_specs=[idx_s, col, tok, gain, tab_s],
        out_specs=coef,
        out_shape=jax.ShapeDtypeStruct((n, PEER_PAIRS), F32),
        scratch_shapes=[tile, pltpu.VMEM((PEER_TOKENS, d_model), F32)],
        compiler_params=_cparams(("arbitrary",)),
        name="peer_act",
    )(idx, gate_t, x1, g, tab)


def _peer_out(idx, c, x1, g, tab):
    n, d_model = x1.shape
    idx_s, col, coef, tok, gain, tab_s, tile = _peer_specs(d_model)
    spread = pltpu.VMEM((PEER_TOKENS, 2 * PEER_PAIRS), F32)
    return pl.pallas_call(
        _peer_out_kernel,
        grid=(n // PEER_TOKENS,),
        in_specs=[idx_s, coef, tok, gain, tab_s],
        out_specs=tok,
        out_shape=jax.ShapeDtypeStruct((n, d_model), F32),
        scratch_shapes=[tile, pltpu.VMEM((PEER_TOKENS, d_model), F32), spread, spread],
        compiler_params=_cparams(("arbitrary",)),
        name="peer_out",
    )(idx, c, x1, g, tab)


CONV_IN = 2 * C_CONV
PROJ_ROWS = 256
CONV_ROWS = 512
CONV_HALO = 32
MIX_ROWS = 256


def _pack_w_in(w_in):
    d = w_in.shape[0]
    c0 = CONV_IN
    c1 = c0 + 3 * RW_WIDTH
    c2, c3, c4 = c1 + LORA_W, c1 + LORA_W + LORA_A, c1 + LORA_W + LORA_A + LORA_G
    pad = lambda w: jnp.zeros((d, LANES), w.dtype).at[:, :w.shape[1]].set(w)
    cols = [w_in[:, :c1], pad(w_in[:, c1:c2]), pad(w_in[:, c2:c3]), pad(w_in[:, c3:c4]), w_in[:, c4:]]
    return jnp.concatenate(cols, axis=1).astype(BF16)


def _in_proj_kernel(x_ref, g_ref, w_ref, conv_ref, rw_ref, gate_ref):
    h = _rms(x_ref[...], g_ref[...]).astype(BF16)
    conv_ref[...] = _dot(h, w_ref[:, 0:CONV_IN])
    rw_ref[...] = _dot(h, w_ref[:, CONV_IN:CONV_IN + RW_IN])
    gate_ref[...] = _dot(h, w_ref[:, CONV_IN + RW_IN:])


def _in_proj(x2, g, w_p):
    n, d = x2.shape
    n_gate = w_p.shape[1] - CONV_IN - RW_IN
    const = lambda a: pl.BlockSpec(a.shape, lambda i: (0, 0))
    rows = lambda w: pl.BlockSpec((PROJ_ROWS, w), lambda i: (i, 0))
    return pl.pallas_call(
        _in_proj_kernel,
        grid=(n // PROJ_ROWS,),
        in_specs=[rows(d), const(g), const(w_p)],
        out_specs=[rows(CONV_IN), rows(RW_IN), rows(n_gate)],
        out_shape=[jax.ShapeDtypeStruct((n, w), F32) for w in (CONV_IN, RW_IN, n_gate)],
        compiler_params=_cparams(("parallel",)),
        name="in_proj",
    )(x2, g, w_p)


def _conv_kernel(u_ref, dw_ref, dwb_ref, lng_ref, lnb_ref, pw_ref, pwb_ref, o_ref, hist_ref):
    @pl.when(pl.program_id(1) == 0)
    def _():
        hist_ref[0:CONV_HALO, :] = jnp.zeros((CONV_HALO, C_CONV), F32)

    u = u_ref[...]
    hist_ref[CONV_HALO:CONV_HALO + CONV_ROWS, :] = u[:, :C_CONV] * _sigmoid(u[:, C_CONV:])
    acc = jnp.zeros((CONV_ROWS, C_CONV), F32) + dwb_ref[...]
    first = CONV_HALO - (CONV_WIDTH - 1)
    for j in range(CONV_WIDTH):
        acc = acc + hist_ref[pl.ds(first + j, CONV_ROWS), :] * dw_ref[j:j + 1, :]
    hist_ref[0:CONV_HALO, :] = hist_ref[CONV_ROWS:CONV_ROWS + CONV_HALO, :]
    mean = jnp.mean(acc, axis=-1, keepdims=True)
    d = acc - mean
    var = jnp.mean(d * d, axis=-1, keepdims=True)
    yn = d * lax.rsqrt(var + LN_EPS) * lng_ref[...] + lnb_ref[...]
    y = (yn * _sigmoid(yn)).astype(BF16)
    o_ref[...] = _dot(y, pw_ref[...]) + pwb_ref[...]


def _conv_branch(u_conv, dw, dw_b, ln_g, ln_b, pw_b16, pw_b):
    b, s, _ = u_conv.shape
    d_model = pw_b16.shape[1]
    const = lambda a: pl.BlockSpec(a.shape, lambda i, j: (0, 0))
    return pl.pallas_call(
        _conv_kernel,
        grid=(b, s // CONV_ROWS),
        in_specs=[pl.BlockSpec((None, CONV_ROWS, CONV_IN), lambda i, j: (i, j, 0)),
                  const(dw), const(dw_b), const(ln_g), const(ln_b), const(pw_b16), const(pw_b)],
        out_specs=pl.BlockSpec((None, CONV_ROWS, d_model), lambda i, j: (i, j, 0)),
        out_shape=jax.ShapeDtypeStruct((b, s, d_model), F32),
        scratch_shapes=[pltpu.VMEM((CONV_HALO + CONV_ROWS, C_CONV), F32)],
        compiler_params=_cparams(("parallel", "arbitrary")),
        name="conv_module",
    )(u_conv, dw, dw_b, ln_g, ln_b, pw_b16, pw_b)


def _mix_kernel(yc_ref, yr_ref, gl_ref, x_ref, wout_ref, g_ref, wq_ref, x1_ref, q_ref):
    d = x_ref.shape[1]
    gl = gl_ref[...]
    mix = _sigmoid(gl[:, :d]) * yc_ref[...] + _sigmoid(gl[:, d:]) * yr_ref[...]
    x1 = x_ref[...] + _dot(mix.astype(BF16), wout_ref[...])
    x1_ref[...] = x1
    h = _rms(x1, g_ref[...]).astype(BF16)
    q_ref[...] = _dot(h, wq_ref[...]).astype(BF16)


def _mix(y_conv, y_rw, gates, x2, w_out_b, g_ffn, wq_b):
    n, d = x2.shape
    const = lambda a: pl.BlockSpec(a.shape, lambda i: (0, 0))
    rows = lambda w: pl.BlockSpec((MIX_ROWS, w), lambda i: (i, 0))
    return pl.pallas_call(
        _mix_kernel,
        grid=(n // MIX_ROWS,),
        in_specs=[rows(d), rows(d), rows(2 * d), rows(d), const(w_out_b), const(g_ffn), const(wq_b)],
        out_specs=[rows(d), rows(wq_b.shape[1])],
        out_shape=[jax.ShapeDtypeStruct((n, d), F32), jax.ShapeDtypeStruct((n, wq_b.shape[1]), BF16)],
        compiler_params=_cparams(("parallel",)),
        name="mix_wq",
    )(y_conv, y_rw, gates, x2, w_out_b, g_ffn, wq_b)


def kernel(x, norm_mix_g, w_in, conv_dw, conv_dw_b, conv_ln_g, conv_ln_b, conv_pw, conv_pw_b,
           rw_mu, rw_w0, rw_w2, rw_a0, rw_a2, rw_g2, rw_k_k, rw_k_a, rw_r_k, rw_gn_g, rw_gn_b,
           rw_w_o, w_out, norm_ffn_g, peer_wq, peer_keys, peer_u, peer_v, norm_f_g):
    assert w_in.shape[0] == 1, "single-layer block"
    b, s, d = x.shape
    n = b * s
    row = lambda a: a.reshape(1, -1)
    x2 = x.reshape(n, d)

    conv_in, rw_in, gates = _in_proj(x2, row(norm_mix_g[0]), _pack_w_in(w_in[0]))
    y_conv = _conv_branch(conv_in.reshape(b, s, -1), conv_dw[0], row(conv_dw_b[0]), row(conv_ln_g[0]),
                          row(conv_ln_b[0]), conv_pw[0].astype(BF16), row(conv_pw_b[0]))
    mu_p, w2p, a2p, g2p = _pack_rwkv_params(rw_mu[0], rw_w2[0], rw_a2[0], rw_g2[0])
    y_rw = _rwkv_branch(rw_in.reshape(b, s, -1), mu_p, row(rw_w0[0]), w2p, row(rw_a0[0]), a2p, g2p,
                        row(rw_k_k[0]), row(rw_k_a[0]), row(rw_r_k[0]), row(rw_gn_g[0]),
                        row(rw_gn_b[0]), rw_w_o[0].astype(BF16), _head_block_ones())
    x1, q = _mix(y_conv.reshape(n, d), y_rw.reshape(n, d), gates, x2, w_out[0].astype(BF16),
                 row(norm_ffn_g[0]), peer_wq[0].astype(BF16))

    idx_t, gate_t = _peer_topk(q, peer_keys[0].astype(BF16))
    idx = idx_t.T
    coef = _peer_act(idx, gate_t, x1, row(norm_ffn_g[0]), _pack_table(peer_u[0]))
    out = _peer_out(idx, coef, x1, row(norm_f_g), _pack_table(peer_v[0]))
    return out.reshape(b, s, d)
```

```python
import functools

import jax
import jax.numpy as jnp
from jax import lax
from jax.experimental import pallas as pl
from jax.experimental.pallas import tpu as pltpu

F32 = jnp.float32
BF16 = jnp.bfloat16
HI = lax.Precision.HIGHEST

LANES = 128
SUBLANES = 8

C_CONV = 512
CONV_WIDTH = 31
RW_HEADS = 8
RW_HEAD_DIM = 64
RW_WIDTH = RW_HEADS * RW_HEAD_DIM
LORA_W, LORA_A, LORA_G = 32, 32, 96
PK_HEADS = 8
PK_HALF = 128
N_KEYS = 128
PK_TOPK = 16
NORM_EPS = 1e-6
LN_EPS = 1e-5
GN_EPS = 64e-5

RW_LW = 3 * RW_WIDTH
RW_LA = RW_LW + LANES
RW_LG = RW_LA + LANES
RW_IN = RW_LG + LANES

CHUNK = 64
RW_PASSES_A = 1
RW_PASSES_T = 1
RW_PASSES_S = 1
VMEM_LIMIT = 56 * 1024 * 1024


def _cparams(sem):
    return pltpu.CompilerParams(dimension_semantics=sem, vmem_limit_bytes=VMEM_LIMIT)


def _dot(a, b, dims=(((1,), (0,)), ((), ())), precision=None):
    return lax.dot_general(a, b, dims, precision=precision, preferred_element_type=F32)


_NN = (((1,), (0,)), ((), ()))
_NT = (((1,), (1,)), ((), ()))
_TN = (((0,), (0,)), ((), ()))


def _mm(a, b, dims, passes):
    if passes == 1:
        return _dot(a.astype(BF16), b.astype(BF16), dims)
    a_hi, a_lo = _split2(a)
    b_hi, b_lo = _split2(b)
    return _dot(a_hi, b_hi, dims) + (_dot(a_hi, b_lo, dims) + _dot(a_lo, b_hi, dims))


def _split2(x):
    hi = x.astype(BF16)
    lo = (x - hi.astype(F32)).astype(BF16)
    return hi, lo


def _split3(x):
    hi = x.astype(BF16)
    r1 = x - hi.astype(F32)
    mid = r1.astype(BF16)
    lo = (r1 - mid.astype(F32)).astype(BF16)
    return hi, mid, lo


def _sigmoid(x):
    return 1.0 / (1.0 + jnp.exp(-x))


def _rwkv_kernel(u_ref, mu_ref, w0_ref, w2_ref, a0_ref, a2_ref, g2_ref, kk_ref, ka_ref,
                 rk_ref, gng_ref, gnb_ref, wo_ref, bd_ref, o_ref, carry_ref, state_ref):
    @pl.when(pl.program_id(0) == 0)
    def _():
        carry_ref[...] = jnp.zeros_like(carry_ref)
        state_ref[...] = jnp.zeros_like(state_ref)

    nb = u_ref.shape[0]
    c = CHUNK
    row = lax.broadcasted_iota(jnp.int32, (c, RW_IN), 0)
    us, prevs = [], []
    for b in range(nb):
        ub = u_ref[b]
        prevs.append(jnp.where(row == 0, carry_ref[b], pltpu.roll(ub, 1, axis=0)))
        carry_ref[b] = ub[c - 1:c, :]
        us.append(ub)
    u = jnp.concatenate(us, axis=0)
    u = u + (jnp.concatenate(prevs, axis=0) - u) * mu_ref[...]

    r = u[:, 0:RW_WIDTH]
    k = u[:, RW_WIDTH:2 * RW_WIDTH]
    v = u[:, 2 * RW_WIDTH:3 * RW_WIDTH]
    lw = u[:, RW_LW:RW_LW + LANES]
    la = u[:, RW_LA:RW_LA + LANES]
    lg = u[:, RW_LG:RW_LG + LANES]

    bd = bd_ref[...]

    def head_sum(x):
        hi, lo = _split2(x)
        return _dot(hi, bd) + _dot(lo, bd)

    z = w0_ref[...] + _dot(jnp.tanh(lw), w2_ref[...])
    sp = jnp.maximum(-z, 0.0) + jnp.log(1.0 + jnp.exp(-jnp.abs(z)))
    wlog = -jnp.exp(-sp - 0.5)
    a_lr = _sigmoid(a0_ref[...] + _dot(la, a2_ref[...]))
    g = _dot(_sigmoid(lg), g2_ref[...])
    kk = k * kk_ref[...]
    kn = kk / jnp.maximum(jnp.sqrt(head_sum(kk * kk)), 1e-12)
    km = k * (1.0 + (a_lr - 1.0) * ka_ref[...])

    ri = lax.broadcasted_iota(jnp.int32, (c, c), 0)
    ci = lax.broadcasted_iota(jnp.int32, (c, c), 1)
    tri = (ri >= ci).astype(BF16)
    w_hi, w_mid, w_lo = _split3(wlog)
    g_incl = jnp.concatenate(
        [_dot(tri, w_hi[b * c:(b + 1) * c]) + _dot(tri, w_mid[b * c:(b + 1) * c])
         + _dot(tri, w_lo[b * c:(b + 1) * c]) for b in range(nb)], axis=0)
    e_incl = jnp.exp(g_incl)
    e_excl = jnp.exp(g_incl - wlog)
    e_neg = jnp.exp(-g_incl)
    r_t = r * e_incl
    a_t = -kn * e_excl
    b_t = kn * a_lr * e_neg
    k_t = km * e_neg

    n2 = 2 * c
    ri2 = lax.broadcasted_iota(jnp.int32, (n2, n2), 0)
    ci2 = lax.broadcasted_iota(jnp.int32, (n2, n2), 1)
    strict = (ri2 % c) > (ci2 % c)
    lower = (ri2 % c) >= (ci2 % c)
    eye = (ri2 == ci2).astype(F32)
    head0 = lax.broadcasted_iota(jnp.int32, (c, LANES), 1) < RW_HEAD_DIM

    def stack2(x):
        return jnp.concatenate([jnp.where(head0, x, 0.0), jnp.where(head0, 0.0, x)], axis=0)

    npair = RW_HEADS // 2
    units = [(b, p) for b in range(nb) for p in range(npair)]
    rows = lambda b: slice(b * c, (b + 1) * c)
    lanes = lambda p: slice(p * LANES, (p + 1) * LANES)
    ops = [[stack2(x[rows(b), lanes(p)]) for x in (a_t, b_t, r_t, k_t, v)] for b, p in units]
    ar = [jnp.concatenate([o[0], o[2]], axis=0) for o in ops]
    gb = [_mm(ar[i], ops[i][1], _NT, RW_PASSES_A) for i in range(len(units))]
    gk = [_mm(ar[i], ops[i][3], _NT, RW_PASSES_A) for i in range(len(units))]
    a_ab = [jnp.where(strict, x[:n2], 0.0) for x in gb]
    a_rb = [jnp.where(lower, x[n2:], 0.0) for x in gb]
    a_kk = [jnp.concatenate([jnp.where(strict, x[:n2], 0.0), jnp.where(lower, x[n2:], 0.0)], axis=0)
            for x in gk]
    pw = a_ab
    t = [eye + x for x in pw]
    for _ in range(5):
        pw = [_mm(x, x, _NN, RW_PASSES_T) for x in pw]
        t = [ti + _mm(ti, x, _NN, RW_PASSES_T) for ti, x in zip(t, pw)]
    st = [state_ref[b * npair + p] for b, p in units]
    wy = [_mm(ar[i], st[i], _NT, RW_PASSES_S) for i in range(len(units))]
    av = [_mm(a_kk[i], ops[i][4], _NN, RW_PASSES_S) for i in range(len(units))]
    uu = [_mm(t[i], wy[i][:n2] + av[i][:n2], _NN, RW_PASSES_S) for i in range(len(units))]
    y2 = [wy[i][n2:] + av[i][n2:] + _mm(a_rb[i], uu[i], _NN, RW_PASSES_S) for i in range(len(units))]
    upd = [_mm(jnp.concatenate([uu[i], ops[i][4]], axis=0), jnp.concatenate([ops[i][1], ops[i][3]], axis=0),
               _TN, RW_PASSES_S) for i in range(len(units))]
    for i, (b, p) in enumerate(units):
        gamma = e_incl[(b + 1) * c - 1:(b + 1) * c, lanes(p)]
        state_ref[b * npair + p] = (st[i] + upd[i]) * gamma
    y = jnp.concatenate(
        [jnp.concatenate([y2[b * npair + p][:c] + y2[b * npair + p][c:] for p in range(npair)], axis=1)
         for b in range(nb)], axis=0)

    inv_n = 1.0 / RW_HEAD_DIM
    mean = head_sum(y) * inv_n
    d = y - mean
    var = head_sum(d * d) * inv_n
    yn = d * lax.rsqrt(var + GN_EPS) * gng_ref[...] + gnb_ref[...]
    bonus = head_sum(r * km * rk_ref[...]) * v
    o = ((yn + bonus) * g).astype(BF16)
    out = _dot(o, wo_ref[...])
    for b in range(nb):
        o_ref[b] = out[b * c:(b + 1) * c]


def _rwkv_branch(u_rw, mu_p, w0, w2p, a0, a2p, g2p, k_k, k_a, r_k, gn_g, gn_b, w_o_b, bd_b):
    b, s, _ = u_rw.shape
    d_model = w_o_b.shape[1]
    row = lambda w: pl.BlockSpec((1, w), lambda j: (0, 0))
    full = lambda a: pl.BlockSpec(a.shape, lambda j: (0,) * a.ndim)
    return pl.pallas_call(
        _rwkv_kernel,
        grid=(s // CHUNK,),
        in_specs=[
            pl.BlockSpec((b, CHUNK, RW_IN), lambda j: (0, j, 0)),
            row(RW_IN), row(RW_WIDTH), full(w2p), row(RW_WIDTH), full(a2p), full(g2p),
            row(RW_WIDTH), row(RW_WIDTH), row(RW_WIDTH), row(RW_WIDTH), row(RW_WIDTH),
            full(w_o_b), full(bd_b),
        ],
        out_specs=pl.BlockSpec((b, CHUNK, d_model), lambda j: (0, j, 0)),
        out_shape=jax.ShapeDtypeStruct((b, s, d_model), F32),
        scratch_shapes=[
            pltpu.VMEM((b, 1, RW_IN), F32),
            pltpu.VMEM((b * RW_HEADS // 2, LANES, LANES), F32),
        ],
        compiler_params=_cparams(("arbitrary",)),
        name="rwkv7_mix",
    )(u_rw, mu_p, w0, w2p, a0, a2p, g2p, k_k, k_a, r_k, gn_g, gn_b, w_o_b, bd_b)


def _pack_rwkv_params(rw_mu, rw_w2, rw_a2, rw_g2):
    mu = jnp.zeros((1, RW_IN), F32)
    mu = mu.at[0, :RW_LW].set(rw_mu[:RW_LW])
    mu = mu.at[0, RW_LW:RW_LW + LORA_W].set(rw_mu[RW_LW:RW_LW + LORA_W])
    mu = mu.at[0, RW_LA:RW_LA + LORA_A].set(rw_mu[RW_LW + LORA_W:RW_LW + LORA_W + LORA_A])
    mu = mu.at[0, RW_LG:RW_LG + LORA_G].set(rw_mu[RW_LW + LORA_W + LORA_A:])
    pad = lambda w: jnp.zeros((LANES, RW_WIDTH), F32).at[:w.shape[0]].set(w)
    return mu, pad(rw_w2), pad(rw_a2), pad(rw_g2)


def _head_block_ones():
    i = jnp.arange(RW_WIDTH) // RW_HEAD_DIM
    return (i[:, None] == i[None, :]).astype(BF16)


TOPK_TOKENS = 256
NEG_INF = float("-inf")


def _top16(s, payload=None):
    rows = s.shape[0]
    iota = lax.broadcasted_iota(jnp.int32, s.shape, 0)
    r16 = lax.broadcasted_iota(jnp.int32, (PK_TOPK, s.shape[1]), 0)
    vals = jnp.zeros((PK_TOPK, s.shape[1]), F32)
    outs = jnp.zeros((PK_TOPK, s.shape[1]), jnp.int32)
    for it in range(PK_TOPK):
        m = jnp.max(s, axis=0, keepdims=True)
        ix = jnp.min(jnp.where(s == m, iota, rows), axis=0, keepdims=True)
        hit = iota == ix
        if payload is None:
            o = ix
        else:
            o = jnp.max(jnp.where(hit, payload, -1), axis=0, keepdims=True)
        vals = jnp.where(r16 == it, m, vals)
        outs = jnp.where(r16 == it, o, outs)
        s = jnp.where(hit, NEG_INF, s)
    return vals, outs


def _topk_kernel(q_ref, keys_ref, idx_ref, gate_ref):
    k1 = keys_ref[0]
    k2 = keys_ref[1]
    for lt in range(TOPK_TOKENS // LANES):
        rows = slice(lt * LANES, (lt + 1) * LANES)
        q = q_ref[rows, :]
        s1 = _dot(k1, q[:, :PK_HALF], _NT)
        s2 = _dot(k2, q[:, PK_HALF:], _NT)
        v1, i1 = _top16(s1)
        v2, i2 = _top16(s2)
        cand = jnp.concatenate([v1[a:a + 1] + v2[:PK_TOPK // (a + 1)] for a in range(PK_TOPK)], axis=0)
        eid = jnp.concatenate([i1[a:a + 1] * N_KEYS + i2[:PK_TOPK // (a + 1)] for a in range(PK_TOPK)], axis=0)
        pad = -cand.shape[0] % SUBLANES
        cand = jnp.concatenate([cand, jnp.full((pad, LANES), NEG_INF, F32)], axis=0)
        eid = jnp.concatenate([eid, jnp.zeros((pad, LANES), jnp.int32)], axis=0)
        best, e = _top16(cand, eid)
        ex = jnp.exp(best - best[0:1])
        gate = ex / jnp.sum(ex, axis=0, keepdims=True)
        idx_ref[:, rows] = e * TABLE_ROWS_PER_EXPERT
        gate_ref[:, rows] = gate


def _peer_topk(q_b, keys_b):
    n = q_b.shape[0]
    hk = PK_HEADS * PK_TOPK
    return pl.pallas_call(
        _topk_kernel,
        grid=(n // TOPK_TOKENS, PK_HEADS),
        in_specs=[
            pl.BlockSpec((TOPK_TOKENS, 2 * PK_HALF), lambda i, h: (i, h)),
            pl.BlockSpec((None, 2, N_KEYS, PK_HALF), lambda i, h: (h, 0, 0, 0)),
        ],
        out_specs=[
            pl.BlockSpec((PK_TOPK, TOPK_TOKENS), lambda i, h: (h, i)),
            pl.BlockSpec((PK_TOPK, TOPK_TOKENS), lambda i, h: (h, i)),
        ],
        out_shape=[jax.ShapeDtypeStruct((hk, n), jnp.int32), jax.ShapeDtypeStruct((hk, n), F32)],
        compiler_params=_cparams(("parallel", "arbitrary")),
        name="peer_topk",
    )(q_b, keys_b)


PEER_TOKENS = 128
PEER_PAIRS = PK_HEADS * PK_TOPK
TABLE_ROWS_PER_EXPERT = 4
TILE_STRIDE = 136


PACK_EXPERTS = 512


def _pack_kernel(t_ref, o_ref):
    half = TABLE_ROWS_PER_EXPERT * LANES
    for j in range(TABLE_ROWS_PER_EXPERT):
        lo = t_ref[:, j * LANES:(j + 1) * LANES]
        hi = t_ref[:, half + j * LANES:half + (j + 1) * LANES]
        o_ref[pl.ds(j, PACK_EXPERTS, stride=TABLE_ROWS_PER_EXPERT), :] = (
            pltpu.pack_elementwise([lo, hi], packed_dtype=BF16))


def _pack_table(tab):
    e, d = tab.shape
    return pl.pallas_call(
        _pack_kernel,
        grid=(e // PACK_EXPERTS,),
        in_specs=[pl.BlockSpec((PACK_EXPERTS, d), lambda i: (i, 0))],
        out_specs=pl.BlockSpec((PACK_EXPERTS * TABLE_ROWS_PER_EXPERT, LANES), lambda i: (i, 0)),
        out_shape=jax.ShapeDtypeStruct((e * TABLE_ROWS_PER_EXPERT, LANES), jnp.uint32),
        compiler_params=_cparams(("parallel",)),
        name="pack_table",
    )(tab)


def _gather_group(idx_ref, offs, g, tab_ref, tile_ref, buf):
    base = pl.multiple_of(g * SUBLANES, SUBLANES)
    for tt in range(SUBLANES):
        for q in range(PEER_PAIRS // len(offs)):
            row = idx_ref.at[base + tt, pl.ds(q * len(offs), len(offs))]
            for r, off in enumerate(offs):
                i = pl.multiple_of(row[off], TABLE_ROWS_PER_EXPERT)
                tile_ref[buf, tt, pl.ds(q * len(offs) + r, TABLE_ROWS_PER_EXPERT, stride=TILE_STRIDE), :] = (
                    tab_ref[pl.ds(i, TABLE_ROWS_PER_EXPERT), :])


def _pipelined_groups(idx_ref, step_ref, tab_ref, tile_ref, compute, carry):
    n_groups = PEER_TOKENS // SUBLANES
    offs = [step_ref[0]]
    for _ in range(SUBLANES - 1):
        offs.append(offs[-1] + step_ref[1])
    _gather_group(idx_ref, offs, 0, tab_ref, tile_ref, 0)

    def two_groups(i, carry):
        g = 2 * i
        _gather_group(idx_ref, offs, g + 1, tab_ref, tile_ref, 1)
        carry = compute(g, 0, carry)
        _gather_group(idx_ref, offs, jnp.minimum(g + 2, n_groups - 1), tab_ref, tile_ref, 0)
        return compute(g + 1, 1, carry)

    return lax.fori_loop(0, n_groups // 2, two_groups, carry)


def _unpack_chunk(tile_ref, j):
    w = tile_ref[pl.ds(j * TILE_STRIDE, PEER_PAIRS), :]
    lo = pltpu.unpack_elementwise(w, index=0, packed_dtype=BF16, unpacked_dtype=F32)
    hi = pltpu.unpack_elementwise(w, index=1, packed_dtype=BF16, unpacked_dtype=F32)
    return lo, hi


def _rms(x, g):
    return x * lax.rsqrt(jnp.mean(x * x, axis=-1, keepdims=True) + NORM_EPS) * g


def _peer_act_kernel(idx_ref, step_ref, gate_ref, x_ref, g_ref, tab_ref, c_ref, tile_ref, h_ref):
    h_ref[...] = _rms(x_ref[...], g_ref[...])
    gate = gate_ref[...]
    lane = lax.broadcasted_iota(jnp.int32, (PEER_PAIRS, PEER_TOKENS), 1)
    half = TABLE_ROWS_PER_EXPERT * LANES

    def compute(g, buf, acc):
        base = pl.multiple_of(g * SUBLANES, SUBLANES)
        h8 = h_ref[pl.ds(base, SUBLANES), :]
        for tt in range(SUBLANES):
            t = base + tt
            tile = tile_ref.at[buf, tt]
            part = jnp.zeros((PEER_PAIRS, LANES), F32)
            for j in range(TABLE_ROWS_PER_EXPERT):
                lo, hi = _unpack_chunk(tile, j)
                xlo = h8[tt:tt + 1, j * LANES:(j + 1) * LANES]
                xhi = h8[tt:tt + 1, half + j * LANES:half + (j + 1) * LANES]
                part = part + lo * xlo + hi * xhi
            act = jnp.sum(part, axis=1, keepdims=True)
            hit = lane == t
            gcol = jnp.sum(jnp.where(hit, gate, 0.0), axis=1, keepdims=True)
            coef = gcol * (0.5 * act * (1.0 + lax.erf(act * (2.0 ** -0.5))))
            acc = jnp.where(hit, coef, acc)
        return acc

    acc = _pipelined_groups(idx_ref, step_ref, tab_ref, tile_ref, compute, jnp.zeros((PEER_PAIRS, PEER_TOKENS), F32))
    c_ref[...] = acc.T


def _peer_out_kernel(idx_ref, step_ref, c_ref, x_ref, g_ref, tab_ref, o_ref, tile_ref, p_ref, cev_ref, cod_ref):
    cb = c_ref[...].astype(BF16)
    pi = lax.broadcasted_iota(jnp.int32, (PEER_PAIRS, 2 * PEER_PAIRS), 0)
    li = lax.broadcasted_iota(jnp.int32, (PEER_PAIRS, 2 * PEER_PAIRS), 1)
    cev_ref[...] = _dot(cb, (li == 2 * pi).astype(BF16))
    cod_ref[...] = _dot(cb, (li == 2 * pi + 1).astype(BF16))
    sub = lax.broadcasted_iota(jnp.int32, (SUBLANES, LANES), 0)
    nch = TABLE_ROWS_PER_EXPERT

    def compute(g, buf, carry):
        base = pl.multiple_of(g * SUBLANES, SUBLANES)
        lhs = jnp.concatenate([cev_ref[pl.ds(base, SUBLANES), :], cod_ref[pl.ds(base, SUBLANES), :]],
                              axis=0).astype(BF16)
        outs = [jnp.zeros((SUBLANES, LANES), F32) for _ in range(2 * nch)]
        for tt in range(SUBLANES):
            tile = tile_ref.at[buf, tt]
            for j in range(nch):
                w = pltpu.bitcast(tile[pl.ds(j * TILE_STRIDE, PEER_PAIRS), :], BF16)
                r = _dot(lhs, w)
                outs[j] = jnp.where(sub == tt, r[:SUBLANES], outs[j])
                outs[nch + j] = jnp.where(sub == tt, r[SUBLANES:], outs[nch + j])
        for cidx, o in enumerate(outs):
            p_ref[pl.ds(base, SUBLANES), cidx * LANES:(cidx + 1) * LANES] = o
        return carry

    _pipelined_groups(idx_ref, step_ref, tab_ref, tile_ref, compute, 0)
    o_ref[...] = _rms(x_ref[...] + p_ref[...], g_ref[...])


def _peer_specs(d_model):
    idx = pl.BlockSpec((PEER_TOKENS, PEER_PAIRS), lambda i: (i, 0), memory_space=pltpu.SMEM)
    col = pl.BlockSpec((PEER_PAIRS, PEER_TOKENS), lambda i: (0, i))
    coef = pl.BlockSpec((PEER_TOKENS, PEER_PAIRS), lambda i: (i, 0))
    tok = pl.BlockSpec((PEER_TOKENS, d_model), lambda i: (i, 0))
    gain = pl.BlockSpec((1, d_model), lambda i: (0, 0))
    tab = pl.BlockSpec(memory_space=pltpu.VMEM)
    tile = pltpu.VMEM((2, SUBLANES, TABLE_ROWS_PER_EXPERT * TILE_STRIDE, LANES), jnp.uint32)
    return idx, col, coef, tok, gain, tab, tile


def _peer_act(idx, gate_t, x1, g, tab):
    n, d_model = x1.shape
    idx_s, col, coef, tok, gain, tab_s, tile = _peer_specs(d_model)
    step_s = pl.BlockSpec(memory_space=pltpu.SMEM)
    step = jnp.arange(2, dtype=jnp.int32)
    return pl.pallas_call(
        _peer_act_kernel,
        grid=(n // PEER_TOKENS,),
        in_specs=[idx_s, step_s, col, tok, gain, tab_s],
        out_specs=coef,
        out_shape=jax.ShapeDtypeStruct((n, PEER_PAIRS), F32),
        scratch_shapes=[tile, pltpu.VMEM((PEER_TOKENS, d_model), F32)],
        compiler_params=_cparams(("arbitrary",)),
        name="peer_act",
    )(idx, step, gate_t, x1, g, tab)


def _peer_out(idx, c, x1, g, tab):
    n, d_model = x1.shape
    idx_s, col, coef, tok, gain, tab_s, tile = _peer_specs(d_model)
    step_s = pl.BlockSpec(memory_space=pltpu.SMEM)
    step = jnp.arange(2, dtype=jnp.int32)
    spread = pltpu.VMEM((PEER_TOKENS, 2 * PEER_PAIRS), F32)
    return pl.pallas_call(
        _peer_out_kernel,
        grid=(n // PEER_TOKENS,),
        in_specs=[idx_s, step_s, coef, tok, gain, tab_s],
        out_specs=tok,
        out_shape=jax.ShapeDtypeStruct((n, d_model), F32),
        scratch_shapes=[tile, pltpu.VMEM((PEER_TOKENS, d_model), F32), spread, spread],
        compiler_params=_cparams(("arbitrary",)),
        name="peer_out",
    )(idx, step, c, x1, g, tab)


CONV_IN = 2 * C_CONV
PROJ_ROWS = 256
CONV_ROWS = 512
CONV_HALO = 32
MIX_ROWS = 256


def _pack_w_in(w_in):
    d = w_in.shape[0]
    c0 = CONV_IN
    c1 = c0 + 3 * RW_WIDTH
    c2, c3, c4 = c1 + LORA_W, c1 + LORA_W + LORA_A, c1 + LORA_W + LORA_A + LORA_G
    pad = lambda w: jnp.zeros((d, LANES), w.dtype).at[:, :w.shape[1]].set(w)
    cols = [w_in[:, :c1], pad(w_in[:, c1:c2]), pad(w_in[:, c2:c3]), pad(w_in[:, c3:c4]), w_in[:, c4:]]
    return jnp.concatenate(cols, axis=1).astype(BF16)


def _in_proj_kernel(x_ref, g_ref, w_ref, conv_ref, rw_ref, gate_ref):
    h = _rms(x_ref[...], g_ref[...]).astype(BF16)
    conv_ref[...] = _dot(h, w_ref[:, 0:CONV_IN])
    rw_ref[...] = _dot(h, w_ref[:, CONV_IN:CONV_IN + RW_IN])
    gate_ref[...] = _dot(h, w_ref[:, CONV_IN + RW_IN:])


def _in_proj(x2, g, w_p):
    n, d = x2.shape
    n_gate = w_p.shape[1] - CONV_IN - RW_IN
    const = lambda a: pl.BlockSpec(a.shape, lambda i: (0, 0))
    rows = lambda w: pl.BlockSpec((PROJ_ROWS, w), lambda i: (i, 0))
    return pl.pallas_call(
        _in_proj_kernel,
        grid=(n // PROJ_ROWS,),
        in_specs=[rows(d), const(g), const(w_p)],
        out_specs=[rows(CONV_IN), rows(RW_IN), rows(n_gate)],
        out_shape=[jax.ShapeDtypeStruct((n, w), F32) for w in (CONV_IN, RW_IN, n_gate)],
        compiler_params=_cparams(("parallel",)),
        name="in_proj",
    )(x2, g, w_p)


def _conv_kernel(u_ref, dw_ref, dwb_ref, lng_ref, lnb_ref, pw_ref, pwb_ref, o_ref, hist_ref):
    @pl.when(pl.program_id(1) == 0)
    def _():
        hist_ref[0:CONV_HALO, :] = jnp.zeros((CONV_HALO, C_CONV), F32)

    u = u_ref[...]
    hist_ref[CONV_HALO:CONV_HALO + CONV_ROWS, :] = u[:, :C_CONV] * _sigmoid(u[:, C_CONV:])
    acc = jnp.zeros((CONV_ROWS, C_CONV), F32) + dwb_ref[...]
    first = CONV_HALO - (CONV_WIDTH - 1)
    for j in range(CONV_WIDTH):
        acc = acc + hist_ref[pl.ds(first + j, CONV_ROWS), :] * dw_ref[j:j + 1, :]
    hist_ref[0:CONV_HALO, :] = hist_ref[CONV_ROWS:CONV_ROWS + CONV_HALO, :]
    mean = jnp.mean(acc, axis=-1, keepdims=True)
    d = acc - mean
    var = jnp.mean(d * d, axis=-1, keepdims=True)
    yn = d * lax.rsqrt(var + LN_EPS) * lng_ref[...] + lnb_ref[...]
    y = (yn * _sigmoid(yn)).astype(BF16)
    o_ref[...] = _dot(y, pw_ref[...]) + pwb_ref[...]


def _conv_branch(u_conv, dw, dw_b, ln_g, ln_b, pw_b16, pw_b):
    b, s, _ = u_conv.shape
    d_model = pw_b16.shape[1]
    const = lambda a: pl.BlockSpec(a.shape, lambda i, j: (0, 0))
    return pl.pallas_call(
        _conv_kernel,
        grid=(b, s // CONV_ROWS),
        in_specs=[pl.BlockSpec((None, CONV_ROWS, CONV_IN), lambda i, j: (i, j, 0)),
                  const(dw), const(dw_b), const(ln_g), const(ln_b), const(pw_b16), const(pw_b)],
        out_specs=pl.BlockSpec((None, CONV_ROWS, d_model), lambda i, j: (i, j, 0)),
        out_shape=jax.ShapeDtypeStruct((b, s, d_model), F32),
        scratch_shapes=[pltpu.VMEM((CONV_HALO + CONV_ROWS, C_CONV), F32)],
        compiler_params=_cparams(("parallel", "arbitrary")),
        name="conv_module",
    )(u_conv, dw, dw_b, ln_g, ln_b, pw_b16, pw_b)


def _mix_kernel(yc_ref, yr_ref, gl_ref, x_ref, wout_ref, g_ref, wq_ref, x1_ref, q_ref):
    d = x_ref.shape[1]
    gl = gl_ref[...]
    mix = _sigmoid(gl[:, :d]) * yc_ref[...] + _sigmoid(gl[:, d:]) * yr_ref[...]
    x1 = x_ref[...] + _dot(mix.astype(BF16), wout_ref[...])
    x1_ref[...] = x1
    h = _rms(x1, g_ref[...]).astype(BF16)
    q_ref[...] = _dot(h, wq_ref[...]).astype(BF16)


def _mix(y_conv, y_rw, gates, x2, w_out_b, g_ffn, wq_b):
    n, d = x2.shape
    const = lambda a: pl.BlockSpec(a.shape, lambda i: (0, 0))
    rows = lambda w: pl.BlockSpec((MIX_ROWS, w), lambda i: (i, 0))
    return pl.pallas_call(
        _mix_kernel,
        grid=(n // MIX_ROWS,),
        in_specs=[rows(d), rows(d), rows(2 * d), rows(d), const(w_out_b), const(g_ffn), const(wq_b)],
        out_specs=[rows(d), rows(wq_b.shape[1])],
        out_shape=[jax.ShapeDtypeStruct((n, d), F32), jax.ShapeDtypeStruct((n, wq_b.shape[1]), BF16)],
        compiler_params=_cparams(("parallel",)),
        name="mix_wq",
    )(y_conv, y_rw, gates, x2, w_out_b, g_ffn, wq_b)


def kernel(x, norm_mix_g, w_in, conv_dw, conv_dw_b, conv_ln_g, conv_ln_b, conv_pw, conv_pw_b,
           rw_mu, rw_w0, rw_w2, rw_a0, rw_a2, rw_g2, rw_k_k, rw_k_a, rw_r_k, rw_gn_g, rw_gn_b,
           rw_w_o, w_out, norm_ffn_g, peer_wq, peer_keys, peer_u, peer_v, norm_f_g):
    assert w_in.shape[0] == 1, "single-layer block"
    b, s, d = x.shape
    n = b * s
    row = lambda a: a.reshape(1, -1)
    x2 = x.reshape(n, d)

    conv_in, rw_in, gates = _in_proj(x2, row(norm_mix_g[0]), _pack_w_in(w_in[0]))
    y_conv = _conv_branch(conv_in.reshape(b, s, -1), conv_dw[0], row(conv_dw_b[0]), row(conv_ln_g[0]),
                          row(conv_ln_b[0]), conv_pw[0].astype(BF16), row(conv_pw_b[0]))
    mu_p, w2p, a2p, g2p = _pack_rwkv_params(rw_mu[0], rw_w2[0], rw_a2[0], rw_g2[0])
    y_rw = _rwkv_branch(rw_in.reshape(b, s, -1), mu_p, row(rw_w0[0]), w2p, row(rw_a0[0]), a2p, g2p,
                        row(rw_k_k[0]), row(rw_k_a[0]), row(rw_r_k[0]), row(rw_gn_g[0]),
                        row(rw_gn_b[0]), rw_w_o[0].astype(BF16), _head_block_ones())
    x1, q = _mix(y_conv.reshape(n, d), y_rw.reshape(n, d), gates, x2, w_out[0].astype(BF16),
                 row(norm_ffn_g[0]), peer_wq[0].astype(BF16))

    idx_t, gate_t = _peer_topk(q, peer_keys[0].astype(BF16))
    idx = idx_t.T
    coef = _peer_act(idx, gate_t, x1, row(norm_ffn_g[0]), _pack_table(peer_u[0]))
    out = _peer_out(idx, coef, x1, row(norm_f_g), _pack_table(peer_v[0]))
    return out.reshape(b, s, d)
```

```python
import functools

import jax
import jax.numpy as jnp
from jax import lax
from jax.experimental import pallas as pl
from jax.experimental.pallas import tpu as pltpu

F32 = jnp.float32
BF16 = jnp.bfloat16
HI = lax.Precision.HIGHEST

LANES = 128
SUBLANES = 8

C_CONV = 512
CONV_WIDTH = 31
RW_HEADS = 8
RW_HEAD_DIM = 64
RW_WIDTH = RW_HEADS * RW_HEAD_DIM
LORA_W, LORA_A, LORA_G = 32, 32, 96
PK_HEADS = 8
PK_HALF = 128
N_KEYS = 128
PK_TOPK = 16
NORM_EPS = 1e-6
LN_EPS = 1e-5
GN_EPS = 64e-5

RW_LW = 3 * RW_WIDTH
RW_LA = RW_LW + LANES
RW_LG = RW_LA + LANES
RW_IN = RW_LG + LANES

CHUNK = 64
RW_PASSES_A = 1
RW_PASSES_T = 1
RW_PASSES_S = 1
VMEM_LIMIT = 56 * 1024 * 1024


def _cparams(sem):
    return pltpu.CompilerParams(dimension_semantics=sem, vmem_limit_bytes=VMEM_LIMIT)


def _dot(a, b, dims=(((1,), (0,)), ((), ())), precision=None):
    return lax.dot_general(a, b, dims, precision=precision, preferred_element_type=F32)


_NN = (((1,), (0,)), ((), ()))
_NT = (((1,), (1,)), ((), ()))
_TN = (((0,), (0,)), ((), ()))


def _mm(a, b, dims, passes):
    if passes == 1:
        return _dot(a.astype(BF16), b.astype(BF16), dims)
    a_hi, a_lo = _split2(a)
    b_hi, b_lo = _split2(b)
    return _dot(a_hi, b_hi, dims) + (_dot(a_hi, b_lo, dims) + _dot(a_lo, b_hi, dims))


def _split2(x):
    hi = x.astype(BF16)
    lo = (x - hi.astype(F32)).astype(BF16)
    return hi, lo


def _split3(x):
    hi = x.astype(BF16)
    r1 = x - hi.astype(F32)
    mid = r1.astype(BF16)
    lo = (r1 - mid.astype(F32)).astype(BF16)
    return hi, mid, lo


def _sigmoid(x):
    return 1.0 / (1.0 + jnp.exp(-x))


def _rwkv_kernel(u_ref, mu_ref, w0_ref, w2_ref, a0_ref, a2_ref, g2_ref, kk_ref, ka_ref,
                 rk_ref, gng_ref, gnb_ref, wo_ref, bd_ref, o_ref, carry_ref, state_ref):
    @pl.when(pl.program_id(0) == 0)
    def _():
        carry_ref[...] = jnp.zeros_like(carry_ref)
        state_ref[...] = jnp.zeros_like(state_ref)

    nb = u_ref.shape[0]
    c = CHUNK
    row = lax.broadcasted_iota(jnp.int32, (c, RW_IN), 0)
    us, prevs = [], []
    for b in range(nb):
        ub = u_ref[b]
        prevs.append(jnp.where(row == 0, carry_ref[b], pltpu.roll(ub, 1, axis=0)))
        carry_ref[b] = ub[c - 1:c, :]
        us.append(ub)
    u = jnp.concatenate(us, axis=0)
    u = u + (jnp.concatenate(prevs, axis=0) - u) * mu_ref[...]

    r = u[:, 0:RW_WIDTH]
    k = u[:, RW_WIDTH:2 * RW_WIDTH]
    v = u[:, 2 * RW_WIDTH:3 * RW_WIDTH]
    lw = u[:, RW_LW:RW_LW + LANES]
    la = u[:, RW_LA:RW_LA + LANES]
    lg = u[:, RW_LG:RW_LG + LANES]

    bd = bd_ref[...]

    def head_sum(x):
        hi, lo = _split2(x)
        return _dot(hi, bd) + _dot(lo, bd)

    z = w0_ref[...] + _dot(jnp.tanh(lw), w2_ref[...])
    sp = jnp.maximum(-z, 0.0) + jnp.log(1.0 + jnp.exp(-jnp.abs(z)))
    wlog = -jnp.exp(-sp - 0.5)
    a_lr = _sigmoid(a0_ref[...] + _dot(la, a2_ref[...]))
    g = _dot(_sigmoid(lg), g2_ref[...])
    kk = k * kk_ref[...]
    kn = kk / jnp.maximum(jnp.sqrt(head_sum(kk * kk)), 1e-12)
    km = k * (1.0 + (a_lr - 1.0) * ka_ref[...])

    ri = lax.broadcasted_iota(jnp.int32, (c, c), 0)
    ci = lax.broadcasted_iota(jnp.int32, (c, c), 1)
    tri = (ri >= ci).astype(BF16)
    w_hi, w_mid, w_lo = _split3(wlog)
    g_incl = jnp.concatenate(
        [_dot(tri, w_hi[b * c:(b + 1) * c]) + _dot(tri, w_mid[b * c:(b + 1) * c])
         + _dot(tri, w_lo[b * c:(b + 1) * c]) for b in range(nb)], axis=0)
    e_incl = jnp.exp(g_incl)
    e_excl = jnp.exp(g_incl - wlog)
    e_neg = jnp.exp(-g_incl)
    r_t = r * e_incl
    a_t = -kn * e_excl
    b_t = kn * a_lr * e_neg
    k_t = km * e_neg

    n2 = 2 * c
    ri2 = lax.broadcasted_iota(jnp.int32, (n2, n2), 0)
    ci2 = lax.broadcasted_iota(jnp.int32, (n2, n2), 1)
    strict = (ri2 % c) > (ci2 % c)
    lower = (ri2 % c) >= (ci2 % c)
    eye = (ri2 == ci2).astype(F32)
    head0 = lax.broadcasted_iota(jnp.int32, (c, LANES), 1) < RW_HEAD_DIM

    def stack2(x):
        return jnp.concatenate([jnp.where(head0, x, 0.0), jnp.where(head0, 0.0, x)], axis=0)

    npair = RW_HEADS // 2
    units = [(b, p) for b in range(nb) for p in range(npair)]
    rows = lambda b: slice(b * c, (b + 1) * c)
    lanes = lambda p: slice(p * LANES, (p + 1) * LANES)
    ops = [[stack2(x[rows(b), lanes(p)]) for x in (a_t, b_t, r_t, k_t, v)] for b, p in units]
    ar = [jnp.concatenate([o[0], o[2]], axis=0) for o in ops]
    gb = [_mm(ar[i], ops[i][1], _NT, RW_PASSES_A) for i in range(len(units))]
    gk = [_mm(ar[i], ops[i][3], _NT, RW_PASSES_A) for i in range(len(units))]
    a_ab = [jnp.where(strict, x[:n2], 0.0) for x in gb]
    a_rb = [jnp.where(lower, x[n2:], 0.0) for x in gb]
    a_kk = [jnp.concatenate([jnp.where(strict, x[:n2], 0.0), jnp.where(lower, x[n2:], 0.0)], axis=0)
            for x in gk]
    pw = a_ab
    t = [eye + x for x in pw]
    for _ in range(5):
        pw = [_mm(x, x, _NN, RW_PASSES_T) for x in pw]
        t = [ti + _mm(ti, x, _NN, RW_PASSES_T) for ti, x in zip(t, pw)]
    st = [state_ref[b * npair + p] for b, p in units]
    wy = [_mm(ar[i], st[i], _NT, RW_PASSES_S) for i in range(len(units))]
    av = [_mm(a_kk[i], ops[i][4], _NN, RW_PASSES_S) for i in range(len(units))]
    uu = [_mm(t[i], wy[i][:n2] + av[i][:n2], _NN, RW_PASSES_S) for i in range(len(units))]
    y2 = [wy[i][n2:] + av[i][n2:] + _mm(a_rb[i], uu[i], _NN, RW_PASSES_S) for i in range(len(units))]
    upd = [_mm(jnp.concatenate([uu[i], ops[i][4]], axis=0), jnp.concatenate([ops[i][1], ops[i][3]], axis=0),
               _TN, RW_PASSES_S) for i in range(len(units))]
    for i, (b, p) in enumerate(units):
        gamma = e_incl[(b + 1) * c - 1:(b + 1) * c, lanes(p)]
        state_ref[b * npair + p] = (st[i] + upd[i]) * gamma
    y = jnp.concatenate(
        [jnp.concatenate([y2[b * npair + p][:c] + y2[b * npair + p][c:] for p in range(npair)], axis=1)
         for b in range(nb)], axis=0)

    inv_n = 1.0 / RW_HEAD_DIM
    mean = head_sum(y) * inv_n
    d = y - mean
    var = head_sum(d * d) * inv_n
    yn = d * lax.rsqrt(var + GN_EPS) * gng_ref[...] + gnb_ref[...]
    bonus = head_sum(r * km * rk_ref[...]) * v
    o = ((yn + bonus) * g).astype(BF16)
    out = _dot(o, wo_ref[...])
    for b in range(nb):
        o_ref[b] = out[b * c:(b + 1) * c].astype(o_ref.dtype)


def _rwkv_branch(u_rw, mu_p, w0, w2p, a0, a2p, g2p, k_k, k_a, r_k, gn_g, gn_b, w_o_b, bd_b):
    b, s, _ = u_rw.shape
    d_model = w_o_b.shape[1]
    row = lambda w: pl.BlockSpec((1, w), lambda j: (0, 0))
    full = lambda a: pl.BlockSpec(a.shape, lambda j: (0,) * a.ndim)
    return pl.pallas_call(
        _rwkv_kernel,
        grid=(s // CHUNK,),
        in_specs=[
            pl.BlockSpec((b, CHUNK, RW_IN), lambda j: (0, j, 0)),
            row(RW_IN), row(RW_WIDTH), full(w2p), row(RW_WIDTH), full(a2p), full(g2p),
            row(RW_WIDTH), row(RW_WIDTH), row(RW_WIDTH), row(RW_WIDTH), row(RW_WIDTH),
            full(w_o_b), full(bd_b),
        ],
        out_specs=pl.BlockSpec((b, CHUNK, d_model), lambda j: (0, j, 0)),
        out_shape=jax.ShapeDtypeStruct((b, s, d_model), BF16),
        scratch_shapes=[
            pltpu.VMEM((b, 1, RW_IN), F32),
            pltpu.VMEM((b * RW_HEADS // 2, LANES, LANES), F32),
        ],
        compiler_params=_cparams(("arbitrary",)),
        name="rwkv7_mix",
    )(u_rw, mu_p, w0, w2p, a0, a2p, g2p, k_k, k_a, r_k, gn_g, gn_b, w_o_b, bd_b)


def _pack_rwkv_params(rw_mu, rw_w2, rw_a2, rw_g2):
    mu = jnp.zeros((1, RW_IN), F32)
    mu = mu.at[0, :RW_LW].set(rw_mu[:RW_LW])
    mu = mu.at[0, RW_LW:RW_LW + LORA_W].set(rw_mu[RW_LW:RW_LW + LORA_W])
    mu = mu.at[0, RW_LA:RW_LA + LORA_A].set(rw_mu[RW_LW + LORA_W:RW_LW + LORA_W + LORA_A])
    mu = mu.at[0, RW_LG:RW_LG + LORA_G].set(rw_mu[RW_LW + LORA_W + LORA_A:])
    pad = lambda w: jnp.zeros((LANES, RW_WIDTH), F32).at[:w.shape[0]].set(w)
    return mu, pad(rw_w2), pad(rw_a2), pad(rw_g2)


def _head_block_ones():
    i = jnp.arange(RW_WIDTH) // RW_HEAD_DIM
    return (i[:, None] == i[None, :]).astype(BF16)


TOPK_TOKENS = 256
NEG_INF = float("-inf")


def _top16(s, payload=None):
    rows = s.shape[0]
    iota = lax.broadcasted_iota(jnp.int32, s.shape, 0)
    r16 = lax.broadcasted_iota(jnp.int32, (PK_TOPK, s.shape[1]), 0)
    vals = jnp.zeros((PK_TOPK, s.shape[1]), F32)
    outs = jnp.zeros((PK_TOPK, s.shape[1]), jnp.int32)
    for it in range(PK_TOPK):
        m = jnp.max(s, axis=0, keepdims=True)
        ix = jnp.min(jnp.where(s == m, iota, rows), axis=0, keepdims=True)
        hit = iota == ix
        if payload is None:
            o = ix
        else:
            o = jnp.max(jnp.where(hit, payload, -1), axis=0, keepdims=True)
        vals = jnp.where(r16 == it, m, vals)
        outs = jnp.where(r16 == it, o, outs)
        s = jnp.where(hit, NEG_INF, s)
    return vals, outs


def _topk_kernel(q_ref, keys_ref, idx_ref, gate_ref):
    k1 = keys_ref[0]
    k2 = keys_ref[1]
    for lt in range(TOPK_TOKENS // LANES):
        rows = slice(lt * LANES, (lt + 1) * LANES)
        q = q_ref[rows, :]
        s1 = _dot(k1, q[:, :PK_HALF], _NT)
        s2 = _dot(k2, q[:, PK_HALF:], _NT)
        v1, i1 = _top16(s1)
        v2, i2 = _top16(s2)
        cand = jnp.concatenate([v1[a:a + 1] + v2[:PK_TOPK // (a + 1)] for a in range(PK_TOPK)], axis=0)
        eid = jnp.concatenate([i1[a:a + 1] * N_KEYS + i2[:PK_TOPK // (a + 1)] for a in range(PK_TOPK)], axis=0)
        pad = -cand.shape[0] % SUBLANES
        cand = jnp.concatenate([cand, jnp.full((pad, LANES), NEG_INF, F32)], axis=0)
        eid = jnp.concatenate([eid, jnp.zeros((pad, LANES), jnp.int32)], axis=0)
        best, e = _top16(cand, eid)
        ex = jnp.exp(best - best[0:1])
        gate = ex / jnp.sum(ex, axis=0, keepdims=True)
        idx_ref[:, rows] = e * TABLE_ROWS_PER_EXPERT
        gate_ref[:, rows] = gate


def _peer_topk(q_b, keys_b):
    n = q_b.shape[0]
    hk = PK_HEADS * PK_TOPK
    return pl.pallas_call(
        _topk_kernel,
        grid=(n // TOPK_TOKENS, PK_HEADS),
        in_specs=[
            pl.BlockSpec((TOPK_TOKENS, 2 * PK_HALF), lambda i, h: (i, h)),
            pl.BlockSpec((None, 2, N_KEYS, PK_HALF), lambda i, h: (h, 0, 0, 0)),
        ],
        out_specs=[
            pl.BlockSpec((PK_TOPK, TOPK_TOKENS), lambda i, h: (h, i)),
            pl.BlockSpec((PK_TOPK, TOPK_TOKENS), lambda i, h: (h, i)),
        ],
        out_shape=[jax.ShapeDtypeStruct((hk, n), jnp.int32), jax.ShapeDtypeStruct((hk, n), F32)],
        compiler_params=_cparams(("parallel", "arbitrary")),
        name="peer_topk",
    )(q_b, keys_b)


PEER_TOKENS = 128
PEER_PAIRS = PK_HEADS * PK_TOPK
TABLE_ROWS_PER_EXPERT = 4
TILE_STRIDE = 136


PACK_EXPERTS = 512


def _pack_kernel(t_ref, o_ref):
    half = TABLE_ROWS_PER_EXPERT * LANES
    for j in range(TABLE_ROWS_PER_EXPERT):
        lo = t_ref[:, j * LANES:(j + 1) * LANES]
        hi = t_ref[:, half + j * LANES:half + (j + 1) * LANES]
        o_ref[pl.ds(j, PACK_EXPERTS, stride=TABLE_ROWS_PER_EXPERT), :] = (
            pltpu.pack_elementwise([lo, hi], packed_dtype=BF16))


def _pack_table(tab):
    e, d = tab.shape
    return pl.pallas_call(
        _pack_kernel,
        grid=(e // PACK_EXPERTS,),
        in_specs=[pl.BlockSpec((PACK_EXPERTS, d), lambda i: (i, 0))],
        out_specs=pl.BlockSpec((PACK_EXPERTS * TABLE_ROWS_PER_EXPERT, LANES), lambda i: (i, 0)),
        out_shape=jax.ShapeDtypeStruct((e * TABLE_ROWS_PER_EXPERT, LANES), jnp.uint32),
        compiler_params=_cparams(("parallel",)),
        name="pack_table",
    )(tab)


def _gather_group(idx_ref, offs, g, tab_ref, tile_ref, buf):
    base = pl.multiple_of(g * SUBLANES, SUBLANES)
    for tt in range(SUBLANES):
        for q in range(PEER_PAIRS // len(offs)):
            row = idx_ref.at[base + tt, pl.ds(q * len(offs), len(offs))]
            for r, off in enumerate(offs):
                i = pl.multiple_of(row[off], TABLE_ROWS_PER_EXPERT)
                tile_ref[buf, tt, pl.ds(q * len(offs) + r, TABLE_ROWS_PER_EXPERT, stride=TILE_STRIDE), :] = (
                    tab_ref[pl.ds(i, TABLE_ROWS_PER_EXPERT), :])


def _pipelined_groups(idx_ref, step_ref, tab_ref, tile_ref, compute, carry):
    n_groups = PEER_TOKENS // SUBLANES
    offs = [step_ref[0]]
    for _ in range(SUBLANES - 1):
        offs.append(offs[-1] + step_ref[1])
    _gather_group(idx_ref, offs, 0, tab_ref, tile_ref, 0)

    def two_groups(i, carry):
        g = 2 * i
        _gather_group(idx_ref, offs, g + 1, tab_ref, tile_ref, 1)
        carry = compute(g, 0, carry)
        _gather_group(idx_ref, offs, g + 2, tab_ref, tile_ref, 0)
        return compute(g + 1, 1, carry)

    carry = lax.fori_loop(0, n_groups // 2 - 1, two_groups, carry)
    _gather_group(idx_ref, offs, n_groups - 1, tab_ref, tile_ref, 1)
    carry = compute(n_groups - 2, 0, carry)
    return compute(n_groups - 1, 1, carry)


def _unpack_chunk(tile_ref, j):
    w = tile_ref[pl.ds(j * TILE_STRIDE, PEER_PAIRS), :]
    lo = pltpu.unpack_elementwise(w, index=0, packed_dtype=BF16, unpacked_dtype=F32)
    hi = pltpu.unpack_elementwise(w, index=1, packed_dtype=BF16, unpacked_dtype=F32)
    return lo, hi


def _rms(x, g):
    return x * lax.rsqrt(jnp.mean(x * x, axis=-1, keepdims=True) + NORM_EPS) * g


def _peer_act_kernel(idx_ref, step_ref, gate_ref, x_ref, g_ref, tab_ref, c_ref, tile_ref, h_ref):
    h_ref[...] = _rms(x_ref[...], g_ref[...])
    gate = gate_ref[...]
    lane = lax.broadcasted_iota(jnp.int32, (PEER_PAIRS, PEER_TOKENS), 1)
    half = TABLE_ROWS_PER_EXPERT * LANES

    def compute(g, buf, acc):
        base = pl.multiple_of(g * SUBLANES, SUBLANES)
        h8 = h_ref[pl.ds(base, SUBLANES), :]
        for tt in range(SUBLANES):
            t = base + tt
            tile = tile_ref.at[buf, tt]
            part = jnp.zeros((PEER_PAIRS, LANES), F32)
            for j in range(TABLE_ROWS_PER_EXPERT):
                lo, hi = _unpack_chunk(tile, j)
                xlo = h8[tt:tt + 1, j * LANES:(j + 1) * LANES]
                xhi = h8[tt:tt + 1, half + j * LANES:half + (j + 1) * LANES]
                part = part + lo * xlo + hi * xhi
            act = jnp.sum(part, axis=1, keepdims=True)
            hit = lane == t
            gcol = jnp.sum(jnp.where(hit, gate, 0.0), axis=1, keepdims=True)
            coef = gcol * (0.5 * act * (1.0 + lax.erf(act * (2.0 ** -0.5))))
            acc = jnp.where(hit, coef, acc)
        return acc

    acc = _pipelined_groups(idx_ref, step_ref, tab_ref, tile_ref, compute, jnp.zeros((PEER_PAIRS, PEER_TOKENS), F32))
    c_ref[...] = acc.T


def _peer_out_kernel(idx_ref, step_ref, c_ref, x_ref, g_ref, tab_ref, o_ref, tile_ref, p_ref, cev_ref, cod_ref):
    cb = c_ref[...].astype(BF16)
    pi = lax.broadcasted_iota(jnp.int32, (PEER_PAIRS, 2 * PEER_PAIRS), 0)
    li = lax.broadcasted_iota(jnp.int32, (PEER_PAIRS, 2 * PEER_PAIRS), 1)
    cev_ref[...] = _dot(cb, (li == 2 * pi).astype(BF16))
    cod_ref[...] = _dot(cb, (li == 2 * pi + 1).astype(BF16))
    sub = lax.broadcasted_iota(jnp.int32, (SUBLANES, LANES), 0)
    nch = TABLE_ROWS_PER_EXPERT

    def compute(g, buf, carry):
        base = pl.multiple_of(g * SUBLANES, SUBLANES)
        lhs = jnp.concatenate([cev_ref[pl.ds(base, SUBLANES), :], cod_ref[pl.ds(base, SUBLANES), :]],
                              axis=0).astype(BF16)
        outs = [jnp.zeros((SUBLANES, LANES), F32) for _ in range(2 * nch)]
        for tt in range(SUBLANES):
            tile = tile_ref.at[buf, tt]
            for j in range(nch):
                w = pltpu.bitcast(tile[pl.ds(j * TILE_STRIDE, PEER_PAIRS), :], BF16)
                r = _dot(lhs, w)
                outs[j] = jnp.where(sub == tt, r[:SUBLANES], outs[j])
                outs[nch + j] = jnp.where(sub == tt, r[SUBLANES:], outs[nch + j])
        for cidx, o in enumerate(outs):
            p_ref[pl.ds(base, SUBLANES), cidx * LANES:(cidx + 1) * LANES] = o
        return carry

    _pipelined_groups(idx_ref, step_ref, tab_ref, tile_ref, compute, 0)
    o_ref[...] = _rms(x_ref[...] + p_ref[...], g_ref[...])


def _peer_specs(d_model):
    idx = pl.BlockSpec((PEER_TOKENS, PEER_PAIRS), lambda i: (i, 0), memory_space=pltpu.SMEM)
    col = pl.BlockSpec((PEER_PAIRS, PEER_TOKENS), lambda i: (0, i))
    coef = pl.BlockSpec((PEER_TOKENS, PEER_PAIRS), lambda i: (i, 0))
    tok = pl.BlockSpec((PEER_TOKENS, d_model), lambda i: (i, 0))
    gain = pl.BlockSpec((1, d_model), lambda i: (0, 0))
    tab = pl.BlockSpec(memory_space=pltpu.VMEM)
    tile = pltpu.VMEM((2, SUBLANES, TABLE_ROWS_PER_EXPERT * TILE_STRIDE, LANES), jnp.uint32)
    return idx, col, coef, tok, gain, tab, tile


def _peer_act(idx, gate_t, x1, g, tab):
    n, d_model = x1.shape
    idx_s, col, coef, tok, gain, tab_s, tile = _peer_specs(d_model)
    step_s = pl.BlockSpec(memory_space=pltpu.SMEM)
    step = jnp.arange(2, dtype=jnp.int32)
    return pl.pallas_call(
        _peer_act_kernel,
        grid=(n // PEER_TOKENS,),
        in_specs=[idx_s, step_s, col, tok, gain, tab_s],
        out_specs=coef,
        out_shape=jax.ShapeDtypeStruct((n, PEER_PAIRS), F32),
        scratch_shapes=[tile, pltpu.VMEM((PEER_TOKENS, d_model), F32)],
        compiler_params=_cparams(("arbitrary",)),
        name="peer_act",
    )(idx, step, gate_t, x1, g, tab)


def _peer_out(idx, c, x1, g, tab):
    n, d_model = x1.shape
    idx_s, col, coef, tok, gain, tab_s, tile = _peer_specs(d_model)
    step_s = pl.BlockSpec(memory_space=pltpu.SMEM)
    step = jnp.arange(2, dtype=jnp.int32)
    spread = pltpu.VMEM((PEER_TOKENS, 2 * PEER_PAIRS), F32)
    return pl.pallas_call(
        _peer_out_kernel,
        grid=(n // PEER_TOKENS,),
        in_specs=[idx_s, step_s, coef, tok, gain, tab_s],
        out_specs=tok,
        out_shape=jax.ShapeDtypeStruct((n, d_model), F32),
        scratch_shapes=[tile, pltpu.VMEM((PEER_TOKENS, d_model), F32), spread, spread],
        compiler_params=_cparams(("arbitrary",)),
        name="peer_out",
    )(idx, step, c, x1, g, tab)


CONV_IN = 2 * C_CONV
PROJ_ROWS = 256
CONV_ROWS = 512
CONV_HALO = 32
CONV_TILE = 32
MIX_ROWS = 256


def _pack_w_in(w_in):
    d = w_in.shape[0]
    c0 = CONV_IN
    c1 = c0 + 3 * RW_WIDTH
    c2, c3, c4 = c1 + LORA_W, c1 + LORA_W + LORA_A, c1 + LORA_W + LORA_A + LORA_G
    pad = lambda w: jnp.zeros((d, LANES), w.dtype).at[:, :w.shape[1]].set(w)
    cols = [w_in[:, :c1], pad(w_in[:, c1:c2]), pad(w_in[:, c2:c3]), pad(w_in[:, c3:c4]), w_in[:, c4:]]
    return jnp.concatenate(cols, axis=1).astype(BF16)


def _in_proj_kernel(x_ref, g_ref, w_ref, conv_ref, rw_ref, gate_ref):
    h = _rms(x_ref[...], g_ref[...]).astype(BF16)
    conv_ref[...] = _dot(h, w_ref[:, 0:CONV_IN]).astype(conv_ref.dtype)
    rw_ref[...] = _dot(h, w_ref[:, CONV_IN:CONV_IN + RW_IN])
    gate_ref[...] = _dot(h, w_ref[:, CONV_IN + RW_IN:]).astype(gate_ref.dtype)


def _in_proj(x2, g, w_p):
    n, d = x2.shape
    n_gate = w_p.shape[1] - CONV_IN - RW_IN
    const = lambda a: pl.BlockSpec(a.shape, lambda i: (0, 0))
    rows = lambda w: pl.BlockSpec((PROJ_ROWS, w), lambda i: (i, 0))
    return pl.pallas_call(
        _in_proj_kernel,
        grid=(n // PROJ_ROWS,),
        in_specs=[rows(d), const(g), const(w_p)],
        out_specs=[rows(CONV_IN), rows(RW_IN), rows(n_gate)],
        out_shape=[jax.ShapeDtypeStruct((n, w), t) for w, t in ((CONV_IN, BF16), (RW_IN, F32), (n_gate, BF16))],
        compiler_params=_cparams(("parallel",)),
        name="in_proj",
    )(x2, g, w_p)


def _conv_kernel(u_ref, dw_ref, dwb_ref, lng_ref, lnb_ref, pw_ref, pwb_ref, o_ref, hist_ref, shift_ref, y_ref):
    @pl.when(pl.program_id(1) == 0)
    def _():
        hist_ref[0:CONV_HALO, :] = jnp.zeros((CONV_HALO, C_CONV), F32)

    u = u_ref[...].astype(F32)
    hist_ref[CONV_HALO:CONV_HALO + CONV_ROWS, :] = u[:, :C_CONV] * _sigmoid(u[:, C_CONV:])
    first = CONV_HALO - (CONV_WIDTH - 1)
    for shift in range(SUBLANES):
        span = CONV_HALO + CONV_ROWS - (SUBLANES if shift else 0)
        shift_ref[shift, 0:span, :] = hist_ref[pl.ds(shift, span), :]
    hist_ref[0:CONV_HALO, :] = hist_ref[CONV_ROWS:CONV_ROWS + CONV_HALO, :]

    def row_tile(rt, carry):
        r0 = pl.multiple_of(rt * CONV_TILE, CONV_TILE)
        acc = jnp.zeros((CONV_TILE, C_CONV), F32) + dwb_ref[...]
        for j in range(CONV_WIDTH):
            shift = (first + j) % SUBLANES
            acc = acc + shift_ref[shift, pl.ds(r0 + (first + j - shift), CONV_TILE), :] * dw_ref[j:j + 1, :]
        y_ref[pl.ds(r0, CONV_TILE), :] = acc
        return carry

    lax.fori_loop(0, CONV_ROWS // CONV_TILE, row_tile, 0)
    acc = y_ref[...]
    mean = jnp.mean(acc, axis=-1, keepdims=True)
    d = acc - mean
    var = jnp.mean(d * d, axis=-1, keepdims=True)
    yn = d * lax.rsqrt(var + LN_EPS) * lng_ref[...] + lnb_ref[...]
    y = (yn * _sigmoid(yn)).astype(BF16)
    o_ref[...] = (_dot(y, pw_ref[...]) + pwb_ref[...]).astype(o_ref.dtype)


def _conv_branch(u_conv, dw, dw_b, ln_g, ln_b, pw_b16, pw_b):
    b, s, _ = u_conv.shape
    d_model = pw_b16.shape[1]
    const = lambda a: pl.BlockSpec(a.shape, lambda i, j: (0, 0))
    return pl.pallas_call(
        _conv_kernel,
        grid=(b, s // CONV_ROWS),
        in_specs=[pl.BlockSpec((None, CONV_ROWS, CONV_IN), lambda i, j: (i, j, 0)),
                  const(dw), const(dw_b), const(ln_g), const(ln_b), const(pw_b16), const(pw_b)],
        out_specs=pl.BlockSpec((None, CONV_ROWS, d_model), lambda i, j: (i, j, 0)),
        out_shape=jax.ShapeDtypeStruct((b, s, d_model), BF16),
        scratch_shapes=[pltpu.VMEM((CONV_HALO + CONV_ROWS, C_CONV), F32),
                        pltpu.VMEM((SUBLANES, CONV_HALO + CONV_ROWS, C_CONV), F32),
                        pltpu.VMEM((CONV_ROWS, C_CONV), F32)],
        compiler_params=_cparams(("parallel", "arbitrary")),
        name="conv_module",
    )(u_conv, dw, dw_b, ln_g, ln_b, pw_b16, pw_b)


def _mix_kernel(yc_ref, yr_ref, gl_ref, x_ref, wout_ref, g_ref, wq_ref, x1_ref, q_ref):
    d = x_ref.shape[1]
    gl = gl_ref[...].astype(F32)
    mix = _sigmoid(gl[:, :d]) * yc_ref[...].astype(F32) + _sigmoid(gl[:, d:]) * yr_ref[...].astype(F32)
    x1 = x_ref[...] + _dot(mix.astype(BF16), wout_ref[...])
    x1_ref[...] = x1
    h = _rms(x1, g_ref[...]).astype(BF16)
    q_ref[...] = _dot(h, wq_ref[...]).astype(BF16)


def _mix(y_conv, y_rw, gates, x2, w_out_b, g_ffn, wq_b):
    n, d = x2.shape
    const = lambda a: pl.BlockSpec(a.shape, lambda i: (0, 0))
    rows = lambda w: pl.BlockSpec((MIX_ROWS, w), lambda i: (i, 0))
    return pl.pallas_call(
        _mix_kernel,
        grid=(n // MIX_ROWS,),
        in_specs=[rows(d), rows(d), rows(2 * d), rows(d), const(w_out_b), const(g_ffn), const(wq_b)],
        out_specs=[rows(d), rows(wq_b.shape[1])],
        out_shape=[jax.ShapeDtypeStruct((n, d), F32), jax.ShapeDtypeStruct((n, wq_b.shape[1]), BF16)],
        compiler_params=_cparams(("parallel",)),
        name="mix_wq",
    )(y_conv, y_rw, gates, x2, w_out_b, g_ffn, wq_b)


def kernel(x, norm_mix_g, w_in, conv_dw, conv_dw_b, conv_ln_g, conv_ln_b, conv_pw, conv_pw_b,
           rw_mu, rw_w0, rw_w2, rw_a0, rw_a2, rw_g2, rw_k_k, rw_k_a, rw_r_k, rw_gn_g, rw_gn_b,
           rw_w_o, w_out, norm_ffn_g, peer_wq, peer_keys, peer_u, peer_v, norm_f_g):
    assert w_in.shape[0] == 1, "single-layer block"
    b, s, d = x.shape
    n = b * s
    row = lambda a: a.reshape(1, -1)
    x2 = x.reshape(n, d)

    conv_in, rw_in, gates = _in_proj(x2, row(norm_mix_g[0]), _pack_w_in(w_in[0]))
    y_conv = _conv_branch(conv_in.reshape(b, s, -1), conv_dw[0], row(conv_dw_b[0]), row(conv_ln_g[0]),
                          row(conv_ln_b[0]), conv_pw[0].astype(BF16), row(conv_pw_b[0]))
    mu_p, w2p, a2p, g2p = _pack_rwkv_params(rw_mu[0], rw_w2[0], rw_a2[0], rw_g2[0])
    y_rw = _rwkv_branch(rw_in.reshape(b, s, -1), mu_p, row(rw_w0[0]), w2p, row(rw_a0[0]), a2p, g2p,
                        row(rw_k_k[0]), row(rw_k_a[0]), row(rw_r_k[0]), row(rw_gn_g[0]),
                        row(rw_gn_b[0]), rw_w_o[0].astype(BF16), _head_block_ones())
    x1, q = _mix(y_conv.reshape(n, d), y_rw.reshape(n, d), gates, x2, w_out[0].astype(BF16),
                 row(norm_ffn_g[0]), peer_wq[0].astype(BF16))

    idx_t, gate_t = _peer_topk(q, peer_keys[0].astype(BF16))
    idx = idx_t.T
    coef = _peer_act(idx, gate_t, x1, row(norm_ffn_g[0]), _pack_table(peer_u[0]))
    out = _peer_out(idx, coef, x1, row(norm_f_g), _pack_table(peer_v[0]))
    return out.reshape(b, s, d)
```

```python
import functools

import jax
import jax.numpy as jnp
from jax import lax
from jax.experimental import pallas as pl
from jax.experimental.pallas import tpu as pltpu

F32 = jnp.float32
BF16 = jnp.bfloat16
HI = lax.Precision.HIGHEST

LANES = 128
SUBLANES = 8

C_CONV = 512
CONV_WIDTH = 31
RW_HEADS = 8
RW_HEAD_DIM = 64
RW_WIDTH = RW_HEADS * RW_HEAD_DIM
LORA_W, LORA_A, LORA_G = 32, 32, 96
PK_HEADS = 8
PK_HALF = 128
N_KEYS = 128
PK_TOPK = 16
NORM_EPS = 1e-6
LN_EPS = 1e-5
GN_EPS = 64e-5

RW_LW = 3 * RW_WIDTH
RW_LA = RW_LW + LANES
RW_LG = RW_LA + LANES
RW_IN = RW_LG + LANES

CHUNK = 64
RW_PASSES_A = 1
RW_PASSES_T = 1
RW_PASSES_S = 1
VMEM_LIMIT = 56 * 1024 * 1024


def _cparams(sem):
    return pltpu.CompilerParams(dimension_semantics=sem, vmem_limit_bytes=VMEM_LIMIT)


def _dot(a, b, dims=(((1,), (0,)), ((), ())), precision=None):
    return lax.dot_general(a, b, dims, precision=precision, preferred_element_type=F32)


_NN = (((1,), (0,)), ((), ()))
_NT = (((1,), (1,)), ((), ()))
_TN = (((0,), (0,)), ((), ()))


def _mm(a, b, dims, passes):
    if passes == 1:
        return _dot(a.astype(BF16), b.astype(BF16), dims)
    a_hi, a_lo = _split2(a)
    b_hi, b_lo = _split2(b)
    return _dot(a_hi, b_hi, dims) + (_dot(a_hi, b_lo, dims) + _dot(a_lo, b_hi, dims))


def _split2(x):
    hi = x.astype(BF16)
    lo = (x - hi.astype(F32)).astype(BF16)
    return hi, lo


def _split3(x):
    hi = x.astype(BF16)
    r1 = x - hi.astype(F32)
    mid = r1.astype(BF16)
    lo = (r1 - mid.astype(F32)).astype(BF16)
    return hi, mid, lo


def _sigmoid(x):
    return 1.0 / (1.0 + jnp.exp(-x))


def _rwkv_kernel(u_ref, mu_ref, w0_ref, w2_ref, a0_ref, a2_ref, g2_ref, kk_ref, ka_ref,
                 rk_ref, gng_ref, gnb_ref, wo_ref, bd_ref, o_ref, carry_ref, state_ref):
    @pl.when(pl.program_id(0) == 0)
    def _():
        carry_ref[...] = jnp.zeros_like(carry_ref)
        state_ref[...] = jnp.zeros_like(state_ref)

    nb = u_ref.shape[0]
    c = CHUNK
    row = lax.broadcasted_iota(jnp.int32, (c, RW_IN), 0)
    us, prevs = [], []
    for b in range(nb):
        ub = u_ref[b]
        prevs.append(jnp.where(row == 0, carry_ref[b], pltpu.roll(ub, 1, axis=0)))
        carry_ref[b] = ub[c - 1:c, :]
        us.append(ub)
    u = jnp.concatenate(us, axis=0)
    u = u + (jnp.concatenate(prevs, axis=0) - u) * mu_ref[...]

    r = u[:, 0:RW_WIDTH]
    k = u[:, RW_WIDTH:2 * RW_WIDTH]
    v = u[:, 2 * RW_WIDTH:3 * RW_WIDTH]
    lw = u[:, RW_LW:RW_LW + LANES]
    la = u[:, RW_LA:RW_LA + LANES]
    lg = u[:, RW_LG:RW_LG + LANES]

    bd = bd_ref[...]

    def head_sum(x):
        hi, lo = _split2(x)
        return _dot(hi, bd) + _dot(lo, bd)

    z = w0_ref[...] + _dot(jnp.tanh(lw), w2_ref[...])
    sp = jnp.maximum(-z, 0.0) + jnp.log(1.0 + jnp.exp(-jnp.abs(z)))
    wlog = -jnp.exp(-sp - 0.5)
    a_lr = _sigmoid(a0_ref[...] + _dot(la, a2_ref[...]))
    g = _dot(_sigmoid(lg), g2_ref[...])
    kk = k * kk_ref[...]
    kn = kk / jnp.maximum(jnp.sqrt(head_sum(kk * kk)), 1e-12)
    km = k * (1.0 + (a_lr - 1.0) * ka_ref[...])

    ri = lax.broadcasted_iota(jnp.int32, (c, c), 0)
    ci = lax.broadcasted_iota(jnp.int32, (c, c), 1)
    tri = (ri >= ci).astype(BF16)
    w_hi, w_mid, w_lo = _split3(wlog)
    g_incl = jnp.concatenate(
        [_dot(tri, w_hi[b * c:(b + 1) * c]) + _dot(tri, w_mid[b * c:(b + 1) * c])
         + _dot(tri, w_lo[b * c:(b + 1) * c]) for b in range(nb)], axis=0)
    e_incl = jnp.exp(g_incl)
    e_excl = jnp.exp(g_incl - wlog)
    e_neg = jnp.exp(-g_incl)
    r_t = r * e_incl
    a_t = -kn * e_excl
    b_t = kn * a_lr * e_neg
    k_t = km * e_neg

    n2 = 2 * c
    ri2 = lax.broadcasted_iota(jnp.int32, (n2, n2), 0)
    ci2 = lax.broadcasted_iota(jnp.int32, (n2, n2), 1)
    strict = (ri2 % c) > (ci2 % c)
    lower = (ri2 % c) >= (ci2 % c)
    eye = (ri2 == ci2).astype(F32)
    head0 = lax.broadcasted_iota(jnp.int32, (c, LANES), 1) < RW_HEAD_DIM

    def stack2(x):
        return jnp.concatenate([jnp.where(head0, x, 0.0), jnp.where(head0, 0.0, x)], axis=0)

    npair = RW_HEADS // 2
    units = [(b, p) for b in range(nb) for p in range(npair)]
    rows = lambda b: slice(b * c, (b + 1) * c)
    lanes = lambda p: slice(p * LANES, (p + 1) * LANES)
    ops = [[stack2(x[rows(b), lanes(p)]) for x in (a_t, b_t, r_t, k_t, v)] for b, p in units]
    ar = [jnp.concatenate([o[0], o[2]], axis=0) for o in ops]
    gb = [_mm(ar[i], ops[i][1], _NT, RW_PASSES_A) for i in range(len(units))]
    gk = [_mm(ar[i], ops[i][3], _NT, RW_PASSES_A) for i in range(len(units))]
    a_ab = [jnp.where(strict, x[:n2], 0.0) for x in gb]
    a_rb = [jnp.where(lower, x[n2:], 0.0) for x in gb]
    a_kk = [jnp.concatenate([jnp.where(strict, x[:n2], 0.0), jnp.where(lower, x[n2:], 0.0)], axis=0)
            for x in gk]
    pw = a_ab
    t = [eye + x for x in pw]
    for _ in range(5):
        pw = [_mm(x, x, _NN, RW_PASSES_T) for x in pw]
        t = [ti + _mm(ti, x, _NN, RW_PASSES_T) for ti, x in zip(t, pw)]
    st = [state_ref[b * npair + p] for b, p in units]
    wy = [_mm(ar[i], st[i], _NT, RW_PASSES_S) for i in range(len(units))]
    av = [_mm(a_kk[i], ops[i][4], _NN, RW_PASSES_S) for i in range(len(units))]
    uu = [_mm(t[i], wy[i][:n2] + av[i][:n2], _NN, RW_PASSES_S) for i in range(len(units))]
    y2 = [wy[i][n2:] + av[i][n2:] + _mm(a_rb[i], uu[i], _NN, RW_PASSES_S) for i in range(len(units))]
    upd = [_mm(jnp.concatenate([uu[i], ops[i][4]], axis=0), jnp.concatenate([ops[i][1], ops[i][3]], axis=0),
               _TN, RW_PASSES_S) for i in range(len(units))]
    for i, (b, p) in enumerate(units):
        gamma = e_incl[(b + 1) * c - 1:(b + 1) * c, lanes(p)]
        state_ref[b * npair + p] = (st[i] + upd[i]) * gamma
    y = jnp.concatenate(
        [jnp.concatenate([y2[b * npair + p][:c] + y2[b * npair + p][c:] for p in range(npair)], axis=1)
         for b in range(nb)], axis=0)

    inv_n = 1.0 / RW_HEAD_DIM
    mean = head_sum(y) * inv_n
    d = y - mean
    var = head_sum(d * d) * inv_n
    yn = d * lax.rsqrt(var + GN_EPS) * gng_ref[...] + gnb_ref[...]
    bonus = head_sum(r * km * rk_ref[...]) * v
    o = ((yn + bonus) * g).astype(BF16)
    out = _dot(o, wo_ref[...])
    for b in range(nb):
        o_ref[b] = out[b * c:(b + 1) * c].astype(o_ref.dtype)


def _rwkv_branch(u_rw, mu_p, w0, w2p, a0, a2p, g2p, k_k, k_a, r_k, gn_g, gn_b, w_o_b, bd_b):
    b, s, _ = u_rw.shape
    d_model = w_o_b.shape[1]
    row = lambda w: pl.BlockSpec((1, w), lambda j: (0, 0))
    full = lambda a: pl.BlockSpec(a.shape, lambda j: (0,) * a.ndim)
    return pl.pallas_call(
        _rwkv_kernel,
        grid=(s // CHUNK,),
        in_specs=[
            pl.BlockSpec((b, CHUNK, RW_IN), lambda j: (0, j, 0)),
            row(RW_IN), row(RW_WIDTH), full(w2p), row(RW_WIDTH), full(a2p), full(g2p),
            row(RW_WIDTH), row(RW_WIDTH), row(RW_WIDTH), row(RW_WIDTH), row(RW_WIDTH),
            full(w_o_b), full(bd_b),
        ],
        out_specs=pl.BlockSpec((b, CHUNK, d_model), lambda j: (0, j, 0)),
        out_shape=jax.ShapeDtypeStruct((b, s, d_model), BF16),
        scratch_shapes=[
            pltpu.VMEM((b, 1, RW_IN), F32),
            pltpu.VMEM((b * RW_HEADS // 2, LANES, LANES), F32),
        ],
        compiler_params=_cparams(("arbitrary",)),
        name="rwkv7_mix",
    )(u_rw, mu_p, w0, w2p, a0, a2p, g2p, k_k, k_a, r_k, gn_g, gn_b, w_o_b, bd_b)


def _pack_rwkv_params(rw_mu, rw_w2, rw_a2, rw_g2):
    mu = jnp.zeros((1, RW_IN), F32)
    mu = mu.at[0, :RW_LW].set(rw_mu[:RW_LW])
    mu = mu.at[0, RW_LW:RW_LW + LORA_W].set(rw_mu[RW_LW:RW_LW + LORA_W])
    mu = mu.at[0, RW_LA:RW_LA + LORA_A].set(rw_mu[RW_LW + LORA_W:RW_LW + LORA_W + LORA_A])
    mu = mu.at[0, RW_LG:RW_LG + LORA_G].set(rw_mu[RW_LW + LORA_W + LORA_A:])
    pad = lambda w: jnp.zeros((LANES, RW_WIDTH), F32).at[:w.shape[0]].set(w)
    return mu, pad(rw_w2), pad(rw_a2), pad(rw_g2)


def _head_block_ones():
    i = jnp.arange(RW_WIDTH) // RW_HEAD_DIM
    return (i[:, None] == i[None, :]).astype(BF16)


TOPK_TOKENS = 256
NEG_INF = float("-inf")


def _top16(s, payload=None):
    rows = s.shape[0]
    iota = lax.broadcasted_iota(jnp.int32, s.shape, 0)
    r16 = lax.broadcasted_iota(jnp.int32, (PK_TOPK, s.shape[1]), 0)
    vals = jnp.zeros((PK_TOPK, s.shape[1]), F32)
    outs = jnp.zeros((PK_TOPK, s.shape[1]), jnp.int32)
    for it in range(PK_TOPK):
        m = jnp.max(s, axis=0, keepdims=True)
        ix = jnp.min(jnp.where(s == m, iota, rows), axis=0, keepdims=True)
        hit = iota == ix
        if payload is None:
            o = ix
        else:
            o = jnp.max(jnp.where(hit, payload, -1), axis=0, keepdims=True)
        vals = jnp.where(r16 == it, m, vals)
        outs = jnp.where(r16 == it, o, outs)
        s = jnp.where(hit, NEG_INF, s)
    return vals, outs


def _topk_kernel(q_ref, keys_ref, idx_ref, gate_ref):
    k1 = keys_ref[0]
    k2 = keys_ref[1]
    for lt in range(TOPK_TOKENS // LANES):
        rows = slice(lt * LANES, (lt + 1) * LANES)
        q = q_ref[rows, :]
        s1 = _dot(k1, q[:, :PK_HALF], _NT)
        s2 = _dot(k2, q[:, PK_HALF:], _NT)
        v1, i1 = _top16(s1)
        v2, i2 = _top16(s2)
        cand = jnp.concatenate([v1[a:a + 1] + v2[:PK_TOPK // (a + 1)] for a in range(PK_TOPK)], axis=0)
        eid = jnp.concatenate([i1[a:a + 1] * N_KEYS + i2[:PK_TOPK // (a + 1)] for a in range(PK_TOPK)], axis=0)
        pad = -cand.shape[0] % SUBLANES
        cand = jnp.concatenate([cand, jnp.full((pad, LANES), NEG_INF, F32)], axis=0)
        eid = jnp.concatenate([eid, jnp.zeros((pad, LANES), jnp.int32)], axis=0)
        best, e = _top16(cand, eid)
        ex = jnp.exp(best - best[0:1])
        gate = ex / jnp.sum(ex, axis=0, keepdims=True)
        idx_ref[:, rows] = e * TABLE_ROWS_PER_EXPERT
        gate_ref[:, rows] = gate


def _peer_topk(q_b, keys_b):
    n = q_b.shape[0]
    hk = PK_HEADS * PK_TOPK
    return pl.pallas_call(
        _topk_kernel,
        grid=(n // TOPK_TOKENS, PK_HEADS),
        in_specs=[
            pl.BlockSpec((TOPK_TOKENS, 2 * PK_HALF), lambda i, h: (i, h)),
            pl.BlockSpec((None, 2, N_KEYS, PK_HALF), lambda i, h: (h, 0, 0, 0)),
        ],
        out_specs=[
            pl.BlockSpec((PK_TOPK, TOPK_TOKENS), lambda i, h: (h, i)),
            pl.BlockSpec((PK_TOPK, TOPK_TOKENS), lambda i, h: (h, i)),
        ],
        out_shape=[jax.ShapeDtypeStruct((hk, n), jnp.int32), jax.ShapeDtypeStruct((hk, n), F32)],
        compiler_params=_cparams(("parallel", "arbitrary")),
        name="peer_topk",
    )(q_b, keys_b)


PEER_TOKENS = 128
PEER_PAIRS = PK_HEADS * PK_TOPK
TABLE_ROWS_PER_EXPERT = 4
TILE_STRIDE = 136


PACK_EXPERTS = 512


def _pack_kernel(t_ref, o_ref):
    half = TABLE_ROWS_PER_EXPERT * LANES
    for j in range(TABLE_ROWS_PER_EXPERT):
        lo = t_ref[:, j * LANES:(j + 1) * LANES]
        hi = t_ref[:, half + j * LANES:half + (j + 1) * LANES]
        o_ref[pl.ds(j, PACK_EXPERTS, stride=TABLE_ROWS_PER_EXPERT), :] = (
            pltpu.pack_elementwise([lo, hi], packed_dtype=BF16))


def _pack_table(tab):
    e, d = tab.shape
    return pl.pallas_call(
        _pack_kernel,
        grid=(e // PACK_EXPERTS,),
        in_specs=[pl.BlockSpec((PACK_EXPERTS, d), lambda i: (i, 0))],
        out_specs=pl.BlockSpec((PACK_EXPERTS * TABLE_ROWS_PER_EXPERT, LANES), lambda i: (i, 0)),
        out_shape=jax.ShapeDtypeStruct((e * TABLE_ROWS_PER_EXPERT, LANES), jnp.uint32),
        compiler_params=_cparams(("parallel",)),
        name="pack_table",
    )(tab)


def _gather_group(idx_ref, offs, g, tab_ref, tile_ref, buf):
    base = pl.multiple_of(g * SUBLANES, SUBLANES)
    for tt in range(SUBLANES):
        for q in range(PEER_PAIRS // len(offs)):
            row = idx_ref.at[base + tt, pl.ds(q * len(offs), len(offs))]
            for r, off in enumerate(offs):
                i = pl.multiple_of(row[off], TABLE_ROWS_PER_EXPERT)
                tile_ref[buf, tt, pl.ds(q * len(offs) + r, TABLE_ROWS_PER_EXPERT, stride=TILE_STRIDE), :] = (
                    tab_ref[pl.ds(i, TABLE_ROWS_PER_EXPERT), :])


def _pipelined_groups(idx_ref, nxt_ref, step_ref, tab_ref, tile_ref, compute, carry):
    n_groups = PEER_TOKENS // SUBLANES
    offs = [step_ref[0]]
    for _ in range(SUBLANES - 1):
        offs.append(offs[-1] + step_ref[1])

    @pl.when(pl.program_id(0) == 0)
    def _():
        _gather_group(idx_ref, offs, 0, tab_ref, tile_ref, 0)

    def two_groups(i, carry):
        g = 2 * i
        _gather_group(idx_ref, offs, g + 1, tab_ref, tile_ref, 1)
        carry = compute(g, 0, carry)
        _gather_group(idx_ref, offs, g + 2, tab_ref, tile_ref, 0)
        return compute(g + 1, 1, carry)

    carry = lax.fori_loop(0, n_groups // 2 - 1, two_groups, carry)
    _gather_group(idx_ref, offs, n_groups - 1, tab_ref, tile_ref, 1)
    carry = compute(n_groups - 2, 0, carry)
    _gather_group(nxt_ref, offs, 0, tab_ref, tile_ref, 0)
    return compute(n_groups - 1, 1, carry)


def _unpack_chunk(tile_ref, j):
    w = tile_ref[pl.ds(j * TILE_STRIDE, PEER_PAIRS), :]
    lo = pltpu.unpack_elementwise(w, index=0, packed_dtype=BF16, unpacked_dtype=F32)
    hi = pltpu.unpack_elementwise(w, index=1, packed_dtype=BF16, unpacked_dtype=F32)
    return lo, hi


def _rms(x, g):
    return x * lax.rsqrt(jnp.mean(x * x, axis=-1, keepdims=True) + NORM_EPS) * g


def _peer_act_kernel(idx_ref, nxt_ref, step_ref, gate_ref, x_ref, g_ref, tab_ref, c_ref, tile_ref, h_ref):
    h_ref[...] = _rms(x_ref[...], g_ref[...])
    gate = gate_ref[...]
    lane = lax.broadcasted_iota(jnp.int32, (PEER_PAIRS, PEER_TOKENS), 1)
    half = TABLE_ROWS_PER_EXPERT * LANES

    def compute(g, buf, acc):
        base = pl.multiple_of(g * SUBLANES, SUBLANES)
        h8 = h_ref[pl.ds(base, SUBLANES), :]
        for tt in range(SUBLANES):
            t = base + tt
            tile = tile_ref.at[buf, tt]
            part = jnp.zeros((PEER_PAIRS, LANES), F32)
            for j in range(TABLE_ROWS_PER_EXPERT):
                lo, hi = _unpack_chunk(tile, j)
                xlo = h8[tt:tt + 1, j * LANES:(j + 1) * LANES]
                xhi = h8[tt:tt + 1, half + j * LANES:half + (j + 1) * LANES]
                part = part + lo * xlo + hi * xhi
            act = jnp.sum(part, axis=1, keepdims=True)
            hit = lane == t
            gcol = jnp.sum(jnp.where(hit, gate, 0.0), axis=1, keepdims=True)
            coef = gcol * (0.5 * act * (1.0 + lax.erf(act * (2.0 ** -0.5))))
            acc = jnp.where(hit, coef, acc)
        return acc

    acc = _pipelined_groups(idx_ref, nxt_ref, step_ref, tab_ref, tile_ref, compute, jnp.zeros((PEER_PAIRS, PEER_TOKENS), F32))
    c_ref[...] = acc.T


def _peer_out_kernel(idx_ref, nxt_ref, step_ref, c_ref, x_ref, g_ref, tab_ref, o_ref, tile_ref, p_ref, cev_ref, cod_ref):
    cb = c_ref[...].astype(BF16)
    pi = lax.broadcasted_iota(jnp.int32, (PEER_PAIRS, 2 * PEER_PAIRS), 0)
    li = lax.broadcasted_iota(jnp.int32, (PEER_PAIRS, 2 * PEER_PAIRS), 1)
    cev_ref[...] = _dot(cb, (li == 2 * pi).astype(BF16))
    cod_ref[...] = _dot(cb, (li == 2 * pi + 1).astype(BF16))
    sub = lax.broadcasted_iota(jnp.int32, (SUBLANES, LANES), 0)
    nch = TABLE_ROWS_PER_EXPERT

    def compute(g, buf, carry):
        base = pl.multiple_of(g * SUBLANES, SUBLANES)
        lhs = jnp.concatenate([cev_ref[pl.ds(base, SUBLANES), :], cod_ref[pl.ds(base, SUBLANES), :]],
                              axis=0).astype(BF16)
        outs = [jnp.zeros((SUBLANES, LANES), F32) for _ in range(2 * nch)]
        for tt in range(SUBLANES):
            tile = tile_ref.at[buf, tt]
            for j in range(nch):
                w = pltpu.bitcast(tile[pl.ds(j * TILE_STRIDE, PEER_PAIRS), :], BF16)
                r = _dot(lhs, w)
                outs[j] = jnp.where(sub == tt, r[:SUBLANES], outs[j])
                outs[nch + j] = jnp.where(sub == tt, r[SUBLANES:], outs[nch + j])
        for cidx, o in enumerate(outs):
            p_ref[pl.ds(base, SUBLANES), cidx * LANES:(cidx + 1) * LANES] = o
        return carry

    _pipelined_groups(idx_ref, nxt_ref, step_ref, tab_ref, tile_ref, compute, 0)
    o_ref[...] = _rms(x_ref[...] + p_ref[...], g_ref[...])


def _peer_specs(d_model):
    idx = pl.BlockSpec((PEER_TOKENS, PEER_PAIRS), lambda i: (i, 0), memory_space=pltpu.SMEM)
    col = pl.BlockSpec((PEER_PAIRS, PEER_TOKENS), lambda i: (0, i))
    coef = pl.BlockSpec((PEER_TOKENS, PEER_PAIRS), lambda i: (i, 0))
    tok = pl.BlockSpec((PEER_TOKENS, d_model), lambda i: (i, 0))
    gain = pl.BlockSpec((1, d_model), lambda i: (0, 0))
    tab = pl.BlockSpec(memory_space=pltpu.VMEM)
    tile = pltpu.VMEM((2, SUBLANES, TABLE_ROWS_PER_EXPERT * TILE_STRIDE, LANES), jnp.uint32)
    return idx, col, coef, tok, gain, tab, tile


def _peer_act(idx, gate_t, x1, g, tab):
    n, d_model = x1.shape
    idx_s, col, coef, tok, gain, tab_s, tile = _peer_specs(d_model)
    step_s = pl.BlockSpec(memory_space=pltpu.SMEM)
    step = jnp.arange(2, dtype=jnp.int32)
    n_steps = n // PEER_TOKENS
    nxt = idx.reshape(n_steps, PEER_TOKENS, PEER_PAIRS)[:, :SUBLANES]
    nxt_s = pl.BlockSpec((None, SUBLANES, PEER_PAIRS), lambda i: (jnp.minimum(i + 1, n_steps - 1), 0, 0),
                         memory_space=pltpu.SMEM)
    return pl.pallas_call(
        _peer_act_kernel,
        grid=(n // PEER_TOKENS,),
        in_specs=[idx_s, nxt_s, step_s, col, tok, gain, tab_s],
        out_specs=coef,
        out_shape=jax.ShapeDtypeStruct((n, PEER_PAIRS), F32),
        scratch_shapes=[tile, pltpu.VMEM((PEER_TOKENS, d_model), F32)],
        compiler_params=_cparams(("arbitrary",)),
        name="peer_act",
    )(idx, nxt, step, gate_t, x1, g, tab)


def _peer_out(idx, c, x1, g, tab):
    n, d_model = x1.shape
    idx_s, col, coef, tok, gain, tab_s, tile = _peer_specs(d_model)
    step_s = pl.BlockSpec(memory_space=pltpu.SMEM)
    step = jnp.arange(2, dtype=jnp.int32)
    n_steps = n // PEER_TOKENS
    nxt = idx.reshape(n_steps, PEER_TOKENS, PEER_PAIRS)[:, :SUBLANES]
    nxt_s = pl.BlockSpec((None, SUBLANES, PEER_PAIRS), lambda i: (jnp.minimum(i + 1, n_steps - 1), 0, 0),
                         memory_space=pltpu.SMEM)
    spread = pltpu.VMEM((PEER_TOKENS, 2 * PEER_PAIRS), F32)
    return pl.pallas_call(
        _peer_out_kernel,
        grid=(n // PEER_TOKENS,),
        in_specs=[idx_s, nxt_s, step_s, coef, tok, gain, tab_s],
        out_specs=tok,
        out_shape=jax.ShapeDtypeStruct((n, d_model), F32),
        scratch_shapes=[tile, pltpu.VMEM((PEER_TOKENS, d_model), F32), spread, spread],
        compiler_params=_cparams(("arbitrary",)),
        name="peer_out",
    )(idx, nxt, step, c, x1, g, tab)


CONV_IN = 2 * C_CONV
PROJ_ROWS = 256
CONV_ROWS = 512
CONV_HALO = 32
CONV_TILE = 32
MIX_ROWS = 256


def _pack_w_in(w_in):
    d = w_in.shape[0]
    c0 = CONV_IN
    c1 = c0 + 3 * RW_WIDTH
    c2, c3, c4 = c1 + LORA_W, c1 + LORA_W + LORA_A, c1 + LORA_W + LORA_A + LORA_G
    pad = lambda w: jnp.zeros((d, LANES), w.dtype).at[:, :w.shape[1]].set(w)
    cols = [w_in[:, :c1], pad(w_in[:, c1:c2]), pad(w_in[:, c2:c3]), pad(w_in[:, c3:c4]), w_in[:, c4:]]
    return jnp.concatenate(cols, axis=1).astype(BF16)


def _in_proj_kernel(x_ref, g_ref, w_ref, conv_ref, rw_ref, gate_ref):
    h = _rms(x_ref[...], g_ref[...]).astype(BF16)
    conv_ref[...] = _dot(h, w_ref[:, 0:CONV_IN]).astype(conv_ref.dtype)
    rw_ref[...] = _dot(h, w_ref[:, CONV_IN:CONV_IN + RW_IN])
    gate_ref[...] = _dot(h, w_ref[:, CONV_IN + RW_IN:]).astype(gate_ref.dtype)


def _in_proj(x2, g, w_p):
    n, d = x2.shape
    n_gate = w_p.shape[1] - CONV_IN - RW_IN
    const = lambda a: pl.BlockSpec(a.shape, lambda i: (0, 0))
    rows = lambda w: pl.BlockSpec((PROJ_ROWS, w), lambda i: (i, 0))
    return pl.pallas_call(
        _in_proj_kernel,
        grid=(n // PROJ_ROWS,),
        in_specs=[rows(d), const(g), const(w_p)],
        out_specs=[rows(CONV_IN), rows(RW_IN), rows(n_gate)],
        out_shape=[jax.ShapeDtypeStruct((n, w), t) for w, t in ((CONV_IN, BF16), (RW_IN, F32), (n_gate, BF16))],
        compiler_params=_cparams(("parallel",)),
        name="in_proj",
    )(x2, g, w_p)


def _conv_kernel(u_ref, dw_ref, dwb_ref, lng_ref, lnb_ref, pw_ref, pwb_ref, o_ref, hist_ref, shift_ref, y_ref):
    @pl.when(pl.program_id(1) == 0)
    def _():
        hist_ref[0:CONV_HALO, :] = jnp.zeros((CONV_HALO, C_CONV), F32)

    u = u_ref[...].astype(F32)
    hist_ref[CONV_HALO:CONV_HALO + CONV_ROWS, :] = u[:, :C_CONV] * _sigmoid(u[:, C_CONV:])
    first = CONV_HALO - (CONV_WIDTH - 1)
    for shift in range(SUBLANES):
        span = CONV_HALO + CONV_ROWS - (SUBLANES if shift else 0)
        shift_ref[shift, 0:span, :] = hist_ref[pl.ds(shift, span), :]
    hist_ref[0:CONV_HALO, :] = hist_ref[CONV_ROWS:CONV_ROWS + CONV_HALO, :]

    def row_tile(rt, carry):
        r0 = pl.multiple_of(rt * CONV_TILE, CONV_TILE)
        acc = jnp.zeros((CONV_TILE, C_CONV), F32) + dwb_ref[...]
        for j in range(CONV_WIDTH):
            shift = (first + j) % SUBLANES
            acc = acc + shift_ref[shift, pl.ds(r0 + (first + j - shift), CONV_TILE), :] * dw_ref[j:j + 1, :]
        y_ref[pl.ds(r0, CONV_TILE), :] = acc
        return carry

    lax.fori_loop(0, CONV_ROWS // CONV_TILE, row_tile, 0)
    acc = y_ref[...]
    mean = jnp.mean(acc, axis=-1, keepdims=True)
    d = acc - mean
    var = jnp.mean(d * d, axis=-1, keepdims=True)
    yn = d * lax.rsqrt(var + LN_EPS) * lng_ref[...] + lnb_ref[...]
    y = (yn * _sigmoid(yn)).astype(BF16)
    o_ref[...] = (_dot(y, pw_ref[...]) + pwb_ref[...]).astype(o_ref.dtype)


def _conv_branch(u_conv, dw, dw_b, ln_g, ln_b, pw_b16, pw_b):
    b, s, _ = u_conv.shape
    d_model = pw_b16.shape[1]
    const = lambda a: pl.BlockSpec(a.shape, lambda i, j: (0, 0))
    return pl.pallas_call(
        _conv_kernel,
        grid=(b, s // CONV_ROWS),
        in_specs=[pl.BlockSpec((None, CONV_ROWS, CONV_IN), lambda i, j: (i, j, 0)),
                  const(dw), const(dw_b), const(ln_g), const(ln_b), const(pw_b16), const(pw_b)],
        out_specs=pl.BlockSpec((None, CONV_ROWS, d_model), lambda i, j: (i, j, 0)),
        out_shape=jax.ShapeDtypeStruct((b, s, d_model), BF16),
        scratch_shapes=[pltpu.VMEM((CONV_HALO + CONV_ROWS, C_CONV), F32),
                        pltpu.VMEM((SUBLANES, CONV_HALO + CONV_ROWS, C_CONV), F32),
                        pltpu.VMEM((CONV_ROWS, C_CONV), F32)],
        compiler_params=_cparams(("parallel", "arbitrary")),
        name="conv_module",
    )(u_conv, dw, dw_b, ln_g, ln_b, pw_b16, pw_b)


def _mix_kernel(yc_ref, yr_ref, gl_ref, x_ref, wout_ref, g_ref, wq_ref, x1_ref, q_ref):
    d = x_ref.shape[1]
    gl = gl_ref[...].astype(F32)
    mix = _sigmoid(gl[:, :d]) * yc_ref[...].astype(F32) + _sigmoid(gl[:, d:]) * yr_ref[...].astype(F32)
    x1 = x_ref[...] + _dot(mix.astype(BF16), wout_ref[...])
    x1_ref[...] = x1
    h = _rms(x1, g_ref[...]).astype(BF16)
    q_ref[...] = _dot(h, wq_ref[...]).astype(BF16)


def _mix(y_conv, y_rw, gates, x2, w_out_b, g_ffn, wq_b):
    n, d = x2.shape
    const = lambda a: pl.BlockSpec(a.shape, lambda i: (0, 0))
    rows = lambda w: pl.BlockSpec((MIX_ROWS, w), lambda i: (i, 0))
    return pl.pallas_call(
        _mix_kernel,
        grid=(n // MIX_ROWS,),
        in_specs=[rows(d), rows(d), rows(2 * d), rows(d), const(w_out_b), const(g_ffn), const(wq_b)],
        out_specs=[rows(d), rows(wq_b.shape[1])],
        out_shape=[jax.ShapeDtypeStruct((n, d), F32), jax.ShapeDtypeStruct((n, wq_b.shape[1]), BF16)],
        compiler_params=_cparams(("parallel",)),
        name="mix_wq",
    )(y_conv, y_rw, gates, x2, w_out_b, g_ffn, wq_b)


def kernel(x, norm_mix_g, w_in, conv_dw, conv_dw_b, conv_ln_g, conv_ln_b, conv_pw, conv_pw_b,
           rw_mu, rw_w0, rw_w2, rw_a0, rw_a2, rw_g2, rw_k_k, rw_k_a, rw_r_k, rw_gn_g, rw_gn_b,
           rw_w_o, w_out, norm_ffn_g, peer_wq, peer_keys, peer_u, peer_v, norm_f_g):
    assert w_in.shape[0] == 1, "single-layer block"
    b, s, d = x.shape
    n = b * s
    row = lambda a: a.reshape(1, -1)
    x2 = x.reshape(n, d)

    conv_in, rw_in, gates = _in_proj(x2, row(norm_mix_g[0]), _pack_w_in(w_in[0]))
    y_conv = _conv_branch(conv_in.reshape(b, s, -1), conv_dw[0], row(conv_dw_b[0]), row(conv_ln_g[0]),
                          row(conv_ln_b[0]), conv_pw[0].astype(BF16), row(conv_pw_b[0]))
    mu_p, w2p, a2p, g2p = _pack_rwkv_params(rw_mu[0], rw_w2[0], rw_a2[0], rw_g2[0])
    y_rw = _rwkv_branch(rw_in.reshape(b, s, -1), mu_p, row(rw_w0[0]), w2p, row(rw_a0[0]), a2p, g2p,
                        row(rw_k_k[0]), row(rw_k_a[0]), row(rw_r_k[0]), row(rw_gn_g[0]),
                        row(rw_gn_b[0]), rw_w_o[0].astype(BF16), _head_block_ones())
    x1, q = _mix(y_conv.reshape(n, d), y_rw.reshape(n, d), gates, x2, w_out[0].astype(BF16),
                 row(norm_ffn_g[0]), peer_wq[0].astype(BF16))

    idx_t, gate_t = _peer_topk(q, peer_keys[0].astype(BF16))
    idx = idx_t.T
    coef = _peer_act(idx, gate_t, x1, row(norm_ffn_g[0]), _pack_table(peer_u[0]))
    out = _peer_out(idx, coef, x1, row(norm_f_g), _pack_table(peer_v[0]))
    return out.reshape(b, s, d)
```
